```python
import math
import jax, jax.numpy as jnp
from jax import lax
import numpy as np

D_MODEL = 2048
BATCH = 2
SEQ = 4096
DEPTH = 4
DEC_BATCH = 8
DEC_SEQ = 4
PAST_LEN = 16384
PAGE_SIZE = 128

HEAD_DIM = 128
MIX_HEADS = D_MODEL // HEAD_DIM
SB_HEADS = (3 * MIX_HEADS) // 8
RET_HEADS = MIX_HEADS // 4
DSA_HEADS = MIX_HEADS - SB_HEADS - RET_HEADS
RET_V_DIM = HEAD_DIM
RET_QK_DIM = HEAD_DIM // 2
IDX_HEADS = 16
IDX_DIM = 64
TOPK_MAX = 256
REL_BUCKETS = 32
REL_MAX_DIST = 128
FFN_DIM = ((8 * D_MODEL // 3 + 127) // 128) * 128
CONV_WIDTH = 3
QBLOCK = 128
RET_CHUNK = 128
RMS_EPS = 1e-6
ROPE_BASE = 10000.0
SPLIT_SIZES = (SB_HEADS * HEAD_DIM, SB_HEADS * HEAD_DIM, SB_HEADS * HEAD_DIM,
               RET_HEADS * RET_QK_DIM, RET_HEADS * RET_QK_DIM, RET_HEADS * RET_V_DIM, RET_HEADS * RET_V_DIM,
               DSA_HEADS * HEAD_DIM, DSA_HEADS * HEAD_DIM, DSA_HEADS * HEAD_DIM,
               IDX_HEADS * IDX_DIM, IDX_HEADS, IDX_DIM)
N_IN = sum(SPLIT_SIZES)

kernel_name = 'hymba_sb_retention_dsa_step'


def _rms(x):
    xf = x.astype(jnp.float32)
    return xf * lax.rsqrt(jnp.mean(xf * xf, axis=-1, keepdims=True) + RMS_EPS)


def rmsnorm(x, g):
    return (_rms(x) * g.astype(jnp.float32)).astype(x.dtype)


def _to_blocks(a, size):
    b, l = a.shape[:2]
    return jnp.moveaxis(a.reshape((b, l // size, size) + a.shape[2:]), 1, 0)


def _from_blocks(a):
    nb, b, sz = a.shape[:3]
    return jnp.moveaxis(a, 0, 1).reshape((b, nb * sz) + a.shape[3:])


def sweep_queries(fn, q_pos, *q_args):
    lq = q_pos.shape[0]
    if lq > QBLOCK and lq % QBLOCK == 0:
        xs = (q_pos.reshape(lq // QBLOCK, QBLOCK),) + tuple(_to_blocks(a, QBLOCK) for a in q_args)
        return _from_blocks(lax.map(lambda t: fn(*t), xs))
    return fn(q_pos, *q_args)


def stick_breaking_block(q_pos, q, k, v):
    f32 = jnp.float32
    k_pos = jnp.arange(k.shape[1])
    z = jnp.einsum('bqhd,bkhd->bhqk', q.astype(f32), k.astype(f32)) * HEAD_DIM ** -0.5
    visible = k_pos[None, :] < q_pos[:, None]
    log_beta = jax.nn.log_sigmoid(z)
    log_keep = jnp.where(visible, jax.nn.log_sigmoid(-z), 0.0)
    log_later = lax.cumsum(log_keep, axis=3, reverse=True) - log_keep
    a = jnp.where(visible, jnp.exp(log_beta + log_later), 0.0)
    return jnp.einsum('bhqk,bkhd->bqhd', a, v.astype(f32)).astype(v.dtype)


def rotary(x, pos):
    half = x.shape[-1] // 2
    inv_freq = 1.0 / (ROPE_BASE ** jnp.linspace(0.0, 1.0, half, dtype=jnp.float32))
    ang = pos.astype(jnp.float32)[:, None] * inv_freq[None, :]
    cos = jnp.cos(ang)[None, :, None, :]
    sin = jnp.sin(ang)[None, :, None, :]
    xf = x.astype(jnp.float32)
    x1, x2 = xf[..., :half], xf[..., half:]
    return jnp.concatenate([x1 * cos - x2 * sin, x1 * sin + x2 * cos], axis=-1)


def retention_chunk(s, q, k, v, log_gamma):
    n = q.shape[1]
    i = jnp.arange(n, dtype=jnp.float32)
    diff = i[:, None] - i[None, :]
    decay = jnp.where(diff >= 0, jnp.exp(log_gamma[:, None, None] * jnp.maximum(diff, 0.0)), 0.0)
    att = jnp.einsum('bihd,bjhd->bhij', q, k) * decay[None]
    inner = jnp.einsum('bhij,bjhe->bihe', att, v)
    q_decay = jnp.exp(log_gamma[None, :] * (i + 1.0)[:, None])[None, :, :, None]
    cross = jnp.einsum('bihd,bhde->bihe', q, s) * q_decay
    k_decay = jnp.exp(log_gamma[None, :] * (n - 1.0 - i)[:, None])[None, :, :, None]
    s_new = jnp.exp(log_gamma * n)[None, :, None, None] * s + jnp.einsum('bjhd,bjhe->bhde', k * k_decay, v)
    return inner + cross, s_new


def retention(s0, q, k, v, log_gamma):
    l = q.shape[1]
    if l > RET_CHUNK and l % RET_CHUNK == 0:
        def step(s, xs):
            o, s = retention_chunk(s, xs[0], xs[1], xs[2], log_gamma)
            return s, o
        s_final, o = lax.scan(step, s0, tuple(_to_blocks(a, RET_CHUNK) for a in (q, k, v)))
        return _from_blocks(o), s_final
    return retention_chunk(s0, q, k, v, log_gamma)


def t5_bucket(dist):
    max_exact = REL_BUCKETS // 2
    dist = jnp.maximum(dist, 0)
    far = max_exact + (jnp.log(jnp.maximum(dist, 1).astype(jnp.float32) / max_exact)
                       / math.log(REL_MAX_DIST / max_exact) * (REL_BUCKETS - max_exact)).astype(jnp.int32)
    return jnp.where(dist < max_exact, dist, jnp.minimum(far, REL_BUCKETS - 1))


def dsa_block(q_pos, q, qi, wi, k, v, ki, rel_bias):
    f32 = jnp.float32
    n_keys = k.shape[1]
    topk = min(TOPK_MAX, n_keys // 4)
    k_pos = jnp.arange(n_keys)
    idx_logits = jnp.einsum('bqhd,bkd->bhqk', qi.astype(f32), ki.astype(f32)) * IDX_DIM ** -0.5
    score = jnp.einsum('bhqk,bqh->bqk', jax.nn.relu(idx_logits), wi.astype(f32) * IDX_HEADS ** -0.5)
    score = jnp.where(k_pos[None, None, :] <= q_pos[None, :, None], score, -jnp.inf)
    _, sel = lax.top_k(score, topk)
    gather = jax.vmap(lambda rows, ids: rows[ids])
    ks = gather(k, sel)
    vs = gather(v, sel)
    logits = jnp.einsum('bqhd,bqkhd->bhqk', q.astype(f32), ks.astype(f32)) * HEAD_DIM ** -0.5
    dist = q_pos[None, :, None] - sel
    bias = rel_bias.astype(f32)[t5_bucket(dist)]
    logits = jnp.where((dist >= 0)[:, None], logits + jnp.moveaxis(bias, 3, 1), -jnp.inf)
    p = jax.nn.softmax(logits, axis=-1)
    return jnp.einsum('bhqk,bqkhd->bqhd', p, vs.astype(f32)).astype(v.dtype)


def trunk_layer(x, pos, past, w_in, w_o, g_attn, g_ffn, g_q, g_k, rel_bias, w_up, conv_w, conv_b, w_down):
    b, l, _ = x.shape
    dt = x.dtype
    f32 = jnp.float32
    h = rmsnorm(x, g_attn)
    proj = jnp.einsum('bld,de->ble', h, w_in)
    split_at = [int(c) for c in np.cumsum(SPLIT_SIZES)[:-1]]
    (sb_q, sb_k, sb_v, r_q, r_k, r_v, r_g, c_q, c_k, c_v, i_q, i_w, i_k) = jnp.split(proj, split_at, axis=-1)

    def heads(a, n):
        return a.reshape(b, l, n, a.shape[-1] // n)

    def with_past(name, new):
        return new if past is None else jnp.concatenate([past[name], new], axis=1)

    sb_k = heads(sb_k, SB_HEADS)
    sb_v = heads(sb_v, SB_HEADS)
    sb_k_all = with_past('sb_k', sb_k)
    sb_v_all = with_past('sb_v', sb_v)
    o_sb = sweep_queries(lambda p, qq: stick_breaking_block(p, qq, sb_k_all, sb_v_all), pos, heads(sb_q, SB_HEADS))

    log_gamma = jnp.log(1.0 - 2.0 ** (-5.0 - jnp.arange(RET_HEADS, dtype=f32)))
    rq = rotary(heads(r_q, RET_HEADS), pos)
    rk = rotary(heads(r_k, RET_HEADS), pos) * RET_QK_DIM ** -0.5
    s0 = jnp.zeros((b, RET_HEADS, RET_QK_DIM, RET_V_DIM), f32) if past is None else past['ret'].astype(f32)
    o_r, s_new = retention(s0, rq, rk, heads(r_v, RET_HEADS).astype(f32), log_gamma)
    o_r = (_rms(o_r) * jax.nn.silu(heads(r_g, RET_HEADS).astype(f32))).astype(dt)

    c_q = (_rms(heads(c_q, DSA_HEADS)) * g_q.astype(f32)).astype(dt)
    c_k = (_rms(heads(c_k, DSA_HEADS)) * g_k.astype(f32)).astype(dt)
    c_v = heads(c_v, DSA_HEADS)
    c_k_all = with_past('dsa_k', c_k)
    c_v_all = with_past('dsa_v', c_v)
    i_k_all = with_past('idx_k', i_k)
    o_c = sweep_queries(lambda p, qq, qi, wi: dsa_block(p, qq, qi, wi, c_k_all, c_v_all, i_k_all, rel_bias),
                        pos, c_q, heads(i_q, IDX_HEADS), i_w)

    mix = jnp.concatenate([o_sb.reshape(b, l, -1), o_r.reshape(b, l, -1), o_c.reshape(b, l, -1)], axis=-1)
    x = x + jnp.einsum('ble,ed->bld', mix, w_o)

    h2 = rmsnorm(x, g_ffn)
    gate, val = jnp.split(jnp.einsum('bld,df->blf', h2, w_up), 2, axis=-1)
    prefix = jnp.zeros((b, CONV_WIDTH - 1, FFN_DIM), dt) if past is None else past['conv'].astype(dt)
    u = jnp.concatenate([prefix, gate], axis=1)
    conv = conv_b + conv_w[0] * u[:, 0:l]
    for i in range(1, CONV_WIDTH):
        conv = conv + conv_w[i] * u[:, i:i + l]
    x = x + jnp.einsum('blf,fd->bld', jax.nn.silu(conv) * val, w_down)
    state = (sb_k, sb_v, c_k, c_v, i_k, s_new.astype(dt), u[:, l:])
    return x, state


def setup_inputs(seed: int = 0) -> dict:
    key = jax.random.key(seed)
    ks = jax.random.split(key, 24)
    f32 = jnp.float32
    n_pages = PAST_LEN // PAGE_SIZE
    n_pool = (DEC_BATCH * n_pages * 5) // 4

    def nrm(k, shape, scale=1.0):
        return jax.random.normal(k, shape, f32) * scale

    page_table = jax.random.permutation(ks[0], n_pool)[: DEC_BATCH * n_pages].reshape(DEC_BATCH, n_pages).astype(jnp.int32)
    return {
        'x_prompt': nrm(ks[1], (BATCH, SEQ, D_MODEL)),
        'x_sample': nrm(ks[2], (DEC_BATCH, DEC_SEQ, D_MODEL)),
        'cache_sb_k': nrm(ks[3], (DEPTH, n_pool, PAGE_SIZE, SB_HEADS, HEAD_DIM)),
        'cache_sb_v': nrm(ks[4], (DEPTH, n_pool, PAGE_SIZE, SB_HEADS, HEAD_DIM)),
        'cache_dsa_k': nrm(ks[5], (DEPTH, n_pool, PAGE_SIZE, DSA_HEADS, HEAD_DIM)),
        'cache_dsa_v': nrm(ks[6], (DEPTH, n_pool, PAGE_SIZE, DSA_HEADS, HEAD_DIM)),
        'cache_idx_k': nrm(ks[7], (DEPTH, n_pool, PAGE_SIZE, IDX_DIM)),
        'state_ret': nrm(ks[8], (DEPTH, DEC_BATCH, RET_HEADS, RET_QK_DIM, RET_V_DIM), 0.5),
        'state_conv': nrm(ks[9], (DEPTH, DEC_BATCH, CONV_WIDTH - 1, FFN_DIM)),
        'page_table': page_table,
        'w_in': nrm(ks[10], (DEPTH, D_MODEL, N_IN), D_MODEL ** -0.5),
        'w_o': nrm(ks[11], (DEPTH, D_MODEL, D_MODEL), D_MODEL ** -0.5),
        'g_attn': 1.0 + nrm(ks[12], (DEPTH, D_MODEL), 0.02),
        'g_ffn': 1.0 + nrm(ks[13], (DEPTH, D_MODEL), 0.02),
        'g_q': 1.0 + nrm(ks[14], (DEPTH, HEAD_DIM), 0.02),
        'g_k': 1.0 + nrm(ks[15], (DEPTH, HEAD_DIM), 0.02),
        'rel_bias': nrm(ks[16], (REL_BUCKETS, DSA_HEADS), 0.5),
        'w_up': nrm(ks[17], (DEPTH, D_MODEL, 2 * FFN_DIM), D_MODEL ** -0.5),
        'conv_w': nrm(ks[18], (DEPTH, CONV_WIDTH, FFN_DIM), CONV_WIDTH ** -0.5),
        'conv_b': nrm(ks[19], (DEPTH, FFN_DIM), 0.02),
        'w_down': nrm(ks[20], (DEPTH, FFN_DIM, D_MODEL), FFN_DIM ** -0.5),
    }


def reference(x_prompt, x_sample, cache_sb_k, cache_sb_v, cache_dsa_k, cache_dsa_v, cache_idx_k,
              state_ret, state_conv, page_table, w_in, w_o, g_attn, g_ffn, g_q, g_k, rel_bias,
              w_up, conv_w, conv_b, w_down):
    n_pages = page_table.shape[1]
    past_len = n_pages * cache_sb_k.shape[2]
    pos_p = jnp.arange(x_prompt.shape[1], dtype=jnp.int32)
    pos_s = past_len + jnp.arange(x_sample.shape[1], dtype=jnp.int32)

    def gather_pages(pool, layer):
        rows = pool[layer, page_table]
        return rows.reshape((rows.shape[0], past_len) + rows.shape[3:])

    yp, ys = x_prompt, x_sample
    new_p, new_s = [], []
    for layer in range(DEPTH):
        weights = (w_in[layer], w_o[layer], g_attn[layer], g_ffn[layer], g_q[layer], g_k[layer], rel_bias,
                   w_up[layer], conv_w[layer], conv_b[layer], w_down[layer])
        yp, st_p = trunk_layer(yp, pos_p, None, *weights)
        past = {'sb_k': gather_pages(cache_sb_k, layer), 'sb_v': gather_pages(cache_sb_v, layer),
                'dsa_k': gather_pages(cache_dsa_k, layer), 'dsa_v': gather_pages(cache_dsa_v, layer),
                'idx_k': gather_pages(cache_idx_k, layer), 'ret': state_ret[layer], 'conv': state_conv[layer]}
        ys, st_s = trunk_layer(ys, pos_s, past, *weights)
        new_p.append(st_p)
        new_s.append(st_s)

    def stk(states, i):
        return jnp.stack([st[i] for st in states])

    return (yp, ys,
            stk(new_p, 0), stk(new_p, 1), stk(new_p, 2), stk(new_p, 3), stk(new_p, 4), stk(new_p, 5), stk(new_p, 6),
            stk(new_s, 0), stk(new_s, 1), stk(new_s, 2), stk(new_s, 3), stk(new_s, 4), stk(new_s, 5), stk(new_s, 6))
```

```python
import functools
import math

import numpy as np
import jax
import jax.numpy as jnp
from jax import lax
from jax.experimental import pallas as pl
from jax.experimental.pallas import tpu as pltpu

F32 = jnp.float32
BF16 = jnp.bfloat16
I32 = jnp.int32

D_MODEL = 2048
HEAD_DIM = 128
SB_HEADS = 6
RET_HEADS = 4
DSA_HEADS = 6
RET_QK = 64
RET_HALF = RET_QK // 2
IDX_HEADS = 16
IDX_DIM = 64
TOPK_MAX = 256
REL_BUCKETS = 32
REL_MAX_DIST = 128
FFN_DIM = 5504
FFN_PAD = 5632
FFN_TILE = 512
RMS_EPS = 1e-6
ROPE_BASE = 10000.0
BLK = 128
SUB = 8
W_SB = 3 * SB_HEADS * HEAD_DIM
W_RET = 2 * RET_HEADS * RET_QK + 2 * RET_HEADS * HEAD_DIM
W_DSA = 3 * DSA_HEADS * HEAD_DIM + IDX_HEADS * IDX_DIM + 2 * BLK
HD_SB = SB_HEADS * HEAD_DIM
HD_DSA = DSA_HEADS * HEAD_DIM
INT_MIN = np.int32(-2 ** 31)
NEG_BIG = -1e30
VMEM_LIMIT = 56 * 1024 * 1024

_NT = (((1,), (1,)), ((), ()))
_TN = (((0,), (0,)), ((), ()))


def _cparams(sem):
    return pltpu.CompilerParams(dimension_semantics=sem, vmem_limit_bytes=VMEM_LIMIT)


def _dot(a, b):
    return jnp.dot(a, b, preferred_element_type=F32)


def _dot_nt(a, b):
    return lax.dot_general(a, b, _NT, preferred_element_type=F32)


def _rms_rows(x):
    return x * lax.rsqrt(jnp.mean(x * x, axis=-1, keepdims=True) + RMS_EPS)


def _t5_bucket_np(dist):
    max_exact = REL_BUCKETS // 2
    dist = np.maximum(dist, 0)
    far = max_exact + (np.log(np.maximum(dist, 1).astype(np.float32) / np.float32(max_exact))
                       / np.float32(math.log(REL_MAX_DIST / max_exact))
                       * np.float32(REL_BUCKETS - max_exact)).astype(np.int32)
    return np.where(dist < max_exact, dist, np.minimum(far, REL_BUCKETS - 1)).astype(np.int32)


def _bucket_tiles(rows, base):
    r = np.arange(rows)[:, None]
    c = np.arange(BLK)[None, :]
    return np.stack([_t5_bucket_np(r - c), _t5_bucket_np(base + r - c)]).astype(np.int32)


def _rope_tables(pos):
    inv_freq = (1.0 / (np.float32(ROPE_BASE) ** np.linspace(0.0, 1.0, RET_HALF, dtype=np.float32))).astype(np.float32)
    ang = pos.astype(np.float32)[:, None] * inv_freq[None, :]
    cos = np.tile(np.cos(ang).astype(np.float32), (1, RET_HEADS))
    sin = np.tile(np.sin(ang).astype(np.float32), (1, RET_HEADS))
    return cos, sin


def _ret_tables(rows, n):
    lg = np.log(np.float32(1.0) - np.float32(2.0) ** (-5.0 - np.arange(RET_HEADS, dtype=np.float32))).astype(np.float32)
    i = np.arange(BLK, dtype=np.float32)
    diff = i[:, None] - i[None, :]
    valid = (np.arange(BLK) < n)
    dec = np.where(diff >= 0, np.exp(lg[:, None, None] * np.maximum(diff, 0.0)), 0.0)
    dec = (dec * valid[None, :, None] * valid[None, None, :])[:, :rows, :]
    qd = np.exp(lg[:, None] * (i + 1.0)[None, :]) * valid[None, :]
    kd = np.exp(lg[:, None] * np.maximum(n - 1.0 - i, 0.0)[None, :]) * valid[None, :]
    gn = np.exp(lg * np.float32(n))
    qd = np.broadcast_to(qd[:, :rows, None], (RET_HEADS, rows, BLK))
    kd = np.broadcast_to(kd[:, :, None], (RET_HEADS, BLK, BLK))
    return (dec.astype(np.float32), np.ascontiguousarray(qd, dtype=np.float32),
            np.ascontiguousarray(kd, dtype=np.float32), [float(g) for g in gn])


def _cumsum_matrix():
    r = np.arange(2 * BLK)[:, None] % BLK
    c = np.arange(2 * BLK)[None, :]
    return np.where(c < BLK, r > c, True).astype(np.float32)


def _prep_in_weights(w):
    perm = np.array([h * RET_QK + a * RET_HALF + i for a in range(2) for h in range(RET_HEADS) for i in range(RET_HALF)])
    w_sb = w[:, :2304]
    w_ret = jnp.concatenate([w[:, 2304:2560][:, perm], w[:, 2560:2816][:, perm], w[:, 2816:3840]], axis=1)
    ik = w[:, 7184:7248]
    w_dsa = jnp.concatenate([w[:, 3840:7168], ik, ik, w[:, 7168:7184],
                             jnp.zeros((w.shape[0], BLK - IDX_HEADS), w.dtype)], axis=1)
    return w_sb.astype(BF16), w_ret.astype(BF16), w_dsa.astype(BF16)


def _pad_cols(a, n):
    return jnp.pad(a, ((0, 0), (0, n - a.shape[1])))


def _norm_in(x_ref, g_ref):
    x = x_ref[...]
    return (_rms_rows(x) * g_ref[...]).astype(BF16)


def _store_heads(ref, a):
    tb, nh, lt, _ = ref.shape
    for h in range(nh):
        ref[:, h] = a[:, h * HEAD_DIM:(h + 1) * HEAD_DIM].reshape(tb, lt, HEAD_DIM)


def _proj_sb_kernel(x_ref, g_ref, w_ref, q_ref, k_ref, kb_ref, v_ref, vb_ref):
    h = _norm_in(x_ref, g_ref)
    q_ref[...] = _dot(h, w_ref[:, 0:HD_SB]).astype(BF16)
    k = _dot(h, w_ref[:, HD_SB:2 * HD_SB])
    _store_heads(k_ref, k)
    kb_ref[...] = k.astype(BF16)
    v = _dot(h, w_ref[:, 2 * HD_SB:3 * HD_SB])
    _store_heads(v_ref, v)
    vb_ref[...] = v.astype(BF16)


def _proj_ret_kernel(x_ref, g_ref, w_ref, q_ref, k_ref, v_ref, gate_ref):
    h = _norm_in(x_ref, g_ref)
    q_ref[...] = _dot(h, w_ref[:, 0:256])
    k_ref[...] = _dot(h, w_ref[:, 256:512])
    v_ref[...] = _dot(h, w_ref[:, 512:1024]).astype(BF16)
    gate_ref[...] = _dot(h, w_ref[:, 1024:1536])


def _proj_dsa_kernel(x_ref, g_ref, w_ref, gq_ref, gk_ref, q_ref, k_ref, kb_ref, v_ref, vb_ref,
                     iq_ref, ik_ref, ik2_ref, iw_ref):
    h = _norm_in(x_ref, g_ref)
    tb, _, lt, _ = k_ref.shape
    for hh in range(DSA_HEADS):
        lo, hi = hh * HEAD_DIM, (hh + 1) * HEAD_DIM
        q_ref[:, lo:hi] = (_rms_rows(_dot(h, w_ref[:, lo:hi])) * gq_ref[...]).astype(BF16)
        k = _rms_rows(_dot(h, w_ref[:, HD_DSA + lo:HD_DSA + hi])) * gk_ref[...]
        k_ref[:, hh] = k.reshape(tb, lt, HEAD_DIM)
        kb_ref[:, lo:hi] = k.astype(BF16)
    v = _dot(h, w_ref[:, 2 * HD_DSA:3 * HD_DSA])
    _store_heads(v_ref, v)
    vb_ref[...] = v.astype(BF16)
    c0 = 3 * HD_DSA
    iq_ref[...] = _dot(h, w_ref[:, c0:c0 + 1024]).astype(BF16)
    ik2 = _dot(h, w_ref[:, c0 + 1024:c0 + 1152])
    ik_ref[...] = ik2[:, 0:IDX_DIM]
    ik2_ref[...] = ik2.astype(BF16)
    iw_ref[...] = _dot(h, w_ref[:, c0 + 1152:c0 + 1280])


def _proj_call(kernel, x, g, w, extra, outs, tm, seq_rows, name):
    t, d = x.shape
    row = lambda i: (i, 0)
    const = lambda i: (0, 0)
    if tm >= seq_rows:
        tb, lt, head_idx = tm // seq_rows, seq_rows, (lambda i: (i, 0, 0, 0))
    else:
        tiles = seq_rows // tm
        tb, lt, head_idx = 1, tm, (lambda i: (i // tiles, 0, i % tiles, 0))
    in_specs = [pl.BlockSpec((tm, d), row), pl.BlockSpec((1, d), const),
                pl.BlockSpec(w.shape, const, pipeline_mode=pl.Buffered(1))]
    in_specs += [pl.BlockSpec(e.shape, const) for e in extra]
    out_specs, out_shape = [], []
    for kind, arg in outs:
        if kind == "heads":
            out_specs.append(pl.BlockSpec((tb, arg, lt, HEAD_DIM), head_idx))
            out_shape.append(jax.ShapeDtypeStruct((t // seq_rows, arg, seq_rows, HEAD_DIM), F32))
        else:
            out_specs.append(pl.BlockSpec((tm, kind), row))
            out_shape.append(jax.ShapeDtypeStruct((t, kind), arg))
    return pl.pallas_call(
        kernel, grid=(t // tm,), in_specs=in_specs, out_specs=out_specs, out_shape=out_shape,
        compiler_params=_cparams(("parallel",)), name=name,
    )(x, g, w, *extra)


def _in_projections(x, g_attn, w_sb, w_ret, w_dsa, g_q, g_k, tm, seq_rows):
    g = g_attn.reshape(1, -1)
    sb = _proj_call(_proj_sb_kernel, x, g, w_sb, [],
                    [(HD_SB, BF16), ("heads", SB_HEADS), (HD_SB, BF16), ("heads", SB_HEADS), (HD_SB, BF16)],
                    tm, seq_rows, "proj_sb")
    ret = _proj_call(_proj_ret_kernel, x, g, w_ret, [],
                     [(256, F32), (256, F32), (512, BF16), (512, F32)], tm, seq_rows, "proj_ret")
    dsa = _proj_call(_proj_dsa_kernel, x, g, w_dsa, [g_q.reshape(1, -1), g_k.reshape(1, -1)],
                     [(HD_DSA, BF16), ("heads", DSA_HEADS), (HD_DSA, BF16), ("heads", DSA_HEADS), (HD_DSA, BF16),
                      (1024, BF16), (IDX_DIM, F32), (BLK, BF16), (BLK, F32)], tm, seq_rows, "proj_dsa")
    return sb, ret, dsa


def _sb_block(q, k, v, u, run, visible):
    z = _dot_nt(q, k) * (HEAD_DIM ** -0.5)
    sp = jnp.maximum(z, 0.0) + jnp.log(1.0 + jnp.exp(-jnp.abs(z)))
    lk = -sp
    if visible is not None:
        lk = jnp.where(visible, lk, 0.0)
    hi = lk.astype(BF16)
    lo = (lk - hi.astype(F32)).astype(BF16)
    cs = _dot(jnp.concatenate([hi, lo], axis=1), u)
    a = jnp.exp(z - sp + cs[:, :BLK] + run)
    if visible is not None:
        a = jnp.where(visible, a, 0.0)
    return _dot(a.astype(BF16), v), cs[:, BLK:]


def _sb_prompt_kernel(q_ref, k_ref, v_ref, u_ref, o_ref, acc_ref, run_ref):
    qi = pl.program_id(1)
    u = u_ref[...]
    acc_ref[...] = jnp.zeros_like(acc_ref)
    run_ref[...] = jnp.zeros_like(run_ref)
    row = lax.broadcasted_iota(I32, (BLK, BLK), 0)
    col = lax.broadcasted_iota(I32, (BLK, BLK), 1)
    strict = col < row

    def step(kb, visible):
        off = pl.multiple_of(kb * BLK, BLK)
        for h in range(SB_HEADS):
            hs = slice(h * HEAD_DIM, (h + 1) * HEAD_DIM)
            o, tot = _sb_block(q_ref[:, hs], k_ref[pl.ds(off, BLK), hs], v_ref[pl.ds(off, BLK), hs], u,
                               run_ref[:, hs], visible)
            acc_ref[:, hs] += o
            run_ref[:, hs] += tot

    step(qi, strict)

    def body(j, carry):
        step(qi - 1 - j, None)
        return carry

    lax.fori_loop(0, qi, body, 0)
    o_ref[...] = acc_ref[...].astype(BF16)


def _sb_prompt(q, k, v, nb, l):
    u = jnp.asarray(_cumsum_matrix(), BF16)
    nq = l // BLK
    return pl.pallas_call(
        _sb_prompt_kernel, grid=(nb, nq),
        in_specs=[pl.BlockSpec((BLK, HD_SB), lambda b, i: (b * nq + i, 0)),
                  pl.BlockSpec((l, HD_SB), lambda b, i: (b, 0)),
                  pl.BlockSpec((l, HD_SB), lambda b, i: (b, 0)),
                  pl.BlockSpec((2 * BLK, 2 * BLK), lambda b, i: (0, 0))],
        out_specs=pl.BlockSpec((BLK, HD_SB), lambda b, i: (b * nq + i, 0)),
        out_shape=jax.ShapeDtypeStruct((nb * l, HD_SB), BF16),
        scratch_shapes=[pltpu.VMEM((BLK, HD_SB), F32), pltpu.VMEM((BLK, HD_SB), F32)],
        compiler_params=_cparams(("parallel", "arbitrary")), name="sb_prompt",
    )(q, k, v, u)


def _sb_decode_kernel(pt_ref, q_ref, kn_ref, vn_ref, u_ref, kp_ref, vp_ref, o_ref,
                      acc_ref, run_ref, kpad_ref, vpad_ref):
    p = pl.program_id(1)
    u = u_ref[...]

    def step(k_pages, v_pages, visible):
        for h in range(SB_HEADS):
            hs = slice(h * HEAD_DIM, (h + 1) * HEAD_DIM)
            o, tot = _sb_block(q_ref[0, :, hs], k_pages[h].astype(BF16), v_pages[h].astype(BF16), u,
                               run_ref[:, hs], visible)
            acc_ref[:, hs] += o
            run_ref[:, hs] += tot

    @pl.when(p == 0)
    def _():
        acc_ref[...] = jnp.zeros_like(acc_ref)
        run_ref[...] = jnp.zeros_like(run_ref)
        kpad_ref[...] = jnp.zeros_like(kpad_ref)
        vpad_ref[...] = jnp.zeros_like(vpad_ref)
        kpad_ref[:, 0:SUB, :] = kn_ref[0]
        vpad_ref[:, 0:SUB, :] = vn_ref[0]
        row = lax.broadcasted_iota(I32, (SUB, BLK), 0)
        col = lax.broadcasted_iota(I32, (SUB, BLK), 1)
        step(kpad_ref, vpad_ref, col < row)

    step(kp_ref, vp_ref, None)

    @pl.when(p == pl.num_programs(1) - 1)
    def _():
        o_ref[0] = acc_ref[...].astype(BF16)


def _sb_decode(page_table, q, kn, vn, cache_k, cache_v, layer):
    nb, npg = page_table.shape
    u = jnp.asarray(_cumsum_matrix(), BF16)
    seq = lambda b, p, pt: (b, 0, 0)
    seq4 = lambda b, p, pt: (b, 0, 0, 0)
    page = lambda b, p, pt: (layer, pt[b * npg + (npg - 1 - p)], 0, 0, 0)
    page_blk = (None, None, SB_HEADS, BLK, HEAD_DIM)
    grid_spec = pltpu.PrefetchScalarGridSpec(
        num_scalar_prefetch=1, grid=(nb, npg),
        in_specs=[pl.BlockSpec((1, SUB, HD_SB), seq), pl.BlockSpec((1, SB_HEADS, SUB, HEAD_DIM), seq4),
                  pl.BlockSpec((1, SB_HEADS, SUB, HEAD_DIM), seq4),
                  pl.BlockSpec((2 * BLK, 2 * BLK), lambda b, p, pt: (0, 0)),
                  pl.BlockSpec(page_blk, page), pl.BlockSpec(page_blk, page)],
        out_specs=pl.BlockSpec((1, SUB, HD_SB), seq),
        scratch_shapes=[pltpu.VMEM((SUB, HD_SB), F32), pltpu.VMEM((SUB, HD_SB), F32),
                        pltpu.VMEM((SB_HEADS, BLK, HEAD_DIM), F32), pltpu.VMEM((SB_HEADS, BLK, HEAD_DIM), F32)])
    return pl.pallas_call(
        _sb_decode_kernel, grid_spec=grid_spec,
        out_shape=jax.ShapeDtypeStruct((nb, SUB, HD_SB), BF16),
        compiler_params=_cparams(("parallel", "arbitrary")), name="sb_decode",
    )(page_table.reshape(-1), q, kn, vn, u, cache_k, cache_v)


def _ret_kernel(q_ref, k_ref, v_ref, g_ref, cos_ref, sin_ref, dec_ref, qd_ref, kd_ref, s0_ref,
                o_ref, sout_ref, s_ref, kc_ref, vp_ref, *, gn):
    c = pl.program_id(1)
    rows = q_ref.shape[0]

    @pl.when(c == 0)
    def _():
        s_ref[...] = jnp.zeros_like(s_ref)
        kc_ref[...] = jnp.zeros_like(kc_ref)
        vp_ref[...] = jnp.zeros_like(vp_ref)
        for h in range(RET_HEADS):
            s_ref[h, h * RET_HALF:(h + 1) * RET_HALF, :] = s0_ref[0, h, 0:RET_HALF, :]
            s_ref[h, BLK + h * RET_HALF:BLK + (h + 1) * RET_HALF, :] = s0_ref[0, h, RET_HALF:RET_QK, :]

    cos = cos_ref[...]
    sin = sin_ref[...]
    q = q_ref[...]
    k = k_ref[...]
    q1 = q[:, :BLK] * cos - q[:, BLK:] * sin
    q2 = q[:, :BLK] * sin + q[:, BLK:] * cos
    k1 = (k[:, :BLK] * cos - k[:, BLK:] * sin) * (RET_QK ** -0.5)
    k2 = (k[:, :BLK] * sin + k[:, BLK:] * cos) * (RET_QK ** -0.5)
    kc_ref[0:rows, :] = jnp.concatenate([k1, k2], axis=1)
    vp_ref[0:rows, :] = v_ref[...].astype(F32)
    kc = kc_ref[...]
    kcat = kc.astype(BF16)
    lane = lax.broadcasted_iota(I32, (1, BLK), 1)
    for h in range(RET_HEADS):
        hs = slice(h * HEAD_DIM, (h + 1) * HEAD_DIM)
        m = jnp.where((lane >= h * RET_HALF) & (lane < (h + 1) * RET_HALF), 1.0, 0.0)
        qm = jnp.concatenate([q1 * m, q2 * m], axis=1).astype(BF16)
        km = (kc * jnp.concatenate([m, m], axis=1)).astype(BF16)
        v = vp_ref[:, hs]
        att = _dot_nt(qm, kcat) * dec_ref[h]
        s = s_ref[h]
        o = _dot(att.astype(BF16), v.astype(BF16)) + _dot(qm, s.astype(BF16)) * qd_ref[h]
        vd = (v * kd_ref[h]).astype(BF16)
        s_ref[h] = gn[h] * s + lax.dot_general(km, vd, _TN, preferred_element_type=F32)
        gate = g_ref[:, hs]
        o_ref[:, hs] = (_rms_rows(o) * (gate / (1.0 + jnp.exp(-gate)))).astype(BF16)

    @pl.when(c == pl.num_programs(1) - 1)
    def _():
        for h in range(RET_HEADS):
            sout_ref[0, h, 0:RET_HALF, :] = s_ref[h, h * RET_HALF:(h + 1) * RET_HALF, :]
            sout_ref[0, h, RET_HALF:RET_QK, :] = s_ref[h, BLK + h * RET_HALF:BLK + (h + 1) * RET_HALF, :]


def _retention(q, k, v, g, s0, pos, rows, n, nb):
    nc = q.shape[0] // (nb * rows)
    cos, sin = _rope_tables(pos)
    dec, qd, kd, gn = _ret_tables(rows, n)
    tok = lambda b, c: (b * nc + c, 0)
    full3 = lambda b, c: (0, 0, 0)
    return pl.pallas_call(
        functools.partial(_ret_kernel, gn=gn), grid=(nb, nc),
        in_specs=[pl.BlockSpec((rows, 256), tok), pl.BlockSpec((rows, 256), tok),
                  pl.BlockSpec((rows, 512), tok), pl.BlockSpec((rows, 512), tok),
                  pl.BlockSpec((rows, BLK), lambda b, c: (c, 0)), pl.BlockSpec((rows, BLK), lambda b, c: (c, 0)),
                  pl.BlockSpec((RET_HEADS, rows, BLK), full3), pl.BlockSpec((RET_HEADS, rows, BLK), full3),
                  pl.BlockSpec((RET_HEADS, BLK, BLK), full3),
                  pl.BlockSpec((1, RET_HEADS, RET_QK, HEAD_DIM), lambda b, c: (b, 0, 0, 0))],
        out_specs=[pl.BlockSpec((rows, 512), tok),
                   pl.BlockSpec((1, RET_HEADS, RET_QK, HEAD_DIM), lambda b, c: (b, 0, 0, 0))],
        out_shape=[jax.ShapeDtypeStruct((q.shape[0], 512), BF16),
                   jax.ShapeDtypeStruct((nb, RET_HEADS, RET_QK, HEAD_DIM), F32)],
        scratch_shapes=[pltpu.VMEM((RET_HEADS, 2 * BLK, HEAD_DIM), F32), pltpu.VMEM((BLK, 2 * BLK), F32),
                        pltpu.VMEM((BLK, RET_HEADS * HEAD_DIM), F32)],
        compiler_params=_cparams(("parallel", "arbitrary")), name="retention",
    )(q, k, v, g, jnp.asarray(cos), jnp.asarray(sin), jnp.asarray(dec), jnp.asarray(qd), jnp.asarray(kd), s0)


def _score_key(score):
    score = jnp.where(score == 0.0, 0.0, score)
    bits = lax.bitcast_convert_type(score, I32)
    return jnp.where(bits < 0, bits ^ np.int32(0x7FFFFFFF), bits)


def _rank_threshold(count_ge, topk, shape):
    ans = jnp.zeros(shape, I32)
    for bit in range(31, -1, -1):
        cand = ans | np.int32(-2 ** 31 if bit == 31 else 2 ** bit)
        ans = jnp.where(count_ge(cand ^ INT_MIN) >= topk, cand, ans)
    return ans ^ INT_MIN


def _bias_tile(bkt, rb_ref, h):
    t = jnp.zeros(bkt.shape, F32)
    for b in range(REL_BUCKETS):
        t = jnp.where(bkt == b, rb_ref[b, h], t)
    return t


def _softmax_block(q, k, v, bias, sel, m_old, l_old):
    s = _dot_nt(q, k) * (HEAD_DIM ** -0.5) + bias
    s = jnp.where(sel, s, NEG_BIG)
    m_new = jnp.maximum(m_old, jnp.max(s, axis=-1, keepdims=True))
    pr = jnp.where(sel, jnp.exp(s - m_new), 0.0)
    alpha = jnp.exp(m_old - m_new)
    l_new = alpha * l_old + jnp.sum(pr, axis=-1, keepdims=True)
    return _dot(pr.astype(BF16), v), alpha, m_new, l_new


def _dsa_prompt_kernel(rb_ref, q_ref, iq_ref, iw_ref, k_ref, v_ref, ik_ref, bkt_ref, o_ref,
                       skey_ref, wib_ref, qm_ref, bias_ref, m_ref, l_ref, acc_ref, *, topk):
    qi = pl.program_id(1)
    row = lax.broadcasted_iota(I32, (BLK, BLK), 0)
    col = lax.broadcasted_iota(I32, (BLK, BLK), 1)
    causal = col <= row
    lane = lax.broadcasted_iota(I32, (1, BLK), 1)

    @pl.when((pl.program_id(0) == 0) & (qi == 0))
    def _():
        for h in range(DSA_HEADS):
            for t in range(2):
                bias_ref[2 * h + t] = _bias_tile(bkt_ref[t], rb_ref, h)

    iw = iw_ref[...]
    for h in range(IDX_HEADS):
        grp = iq_ref[:, (h // 2) * BLK:(h // 2 + 1) * BLK]
        half = (lane < IDX_DIM) if h % 2 == 0 else (lane >= IDX_DIM)
        qm_ref[h * BLK:(h + 1) * BLK, :] = jnp.where(half, grp, jnp.zeros_like(grp))
        wib_ref[h] = jnp.broadcast_to(iw[:, h:h + 1] * (IDX_HEADS ** -0.5), (BLK, BLK))

    def score_block(kb, diag):
        off = pl.multiple_of(kb * BLK, BLK)
        logits = _dot_nt(qm_ref[...], ik_ref[pl.ds(off, BLK), :]) * (IDX_DIM ** -0.5)
        score = jnp.zeros((BLK, BLK), F32)
        for h in range(IDX_HEADS):
            score = score + jnp.maximum(logits[h * BLK:(h + 1) * BLK, :], 0.0) * wib_ref[h]
        key = _score_key(score)
        if diag:
            key = jnp.where(causal, key, INT_MIN)
        skey_ref[kb] = key

    score_block(qi, True)

    def score_body(kb, carry):
        score_block(kb, False)
        return carry

    lax.fori_loop(0, qi, score_body, 0)

    ones = jnp.ones((BLK, BLK), BF16)

    def count_ge(cand):
        def body(kb, cnt):
            return cnt + _dot(jnp.where(skey_ref[kb] >= cand, 1.0, 0.0).astype(BF16), ones)
        return lax.fori_loop(0, qi + 1, body, jnp.zeros((BLK, BLK), F32))

    thr = _rank_threshold(count_ge, float(topk), (BLK, BLK))

    m_ref[...] = jnp.full_like(m_ref, NEG_BIG)
    l_ref[...] = jnp.zeros_like(l_ref)
    acc_ref[...] = jnp.zeros_like(acc_ref)

    def attend(kb, tile, diag):
        off = pl.multiple_of(kb * BLK, BLK)
        sel = skey_ref[kb] >= thr
        if diag:
            sel = sel & causal
        for h in range(DSA_HEADS):
            hs = slice(h * HEAD_DIM, (h + 1) * HEAD_DIM)
            bias = rb_ref[REL_BUCKETS - 1, h] if tile is None else bias_ref[2 * h + tile]
            pv, alpha, m_new, l_new = _softmax_block(q_ref[:, hs], k_ref[pl.ds(off, BLK), hs],
                                                     v_ref[pl.ds(off, BLK), hs], bias, sel, m_ref[h], l_ref[h])
            acc_ref[:, hs] = alpha * acc_ref[:, hs] + pv
            m_ref[h] = m_new
            l_ref[h] = l_new

    attend(qi, 0, True)

    @pl.when(qi >= 1)
    def _():
        attend(qi - 1, 1, False)

    def far_body(kb, carry):
        attend(kb, None, False)
        return carry

    lax.fori_loop(0, qi - 1, far_body, 0)

    for h in range(DSA_HEADS):
        hs = slice(h * HEAD_DIM, (h + 1) * HEAD_DIM)
        o_ref[:, hs] = (acc_ref[:, hs] / l_ref[h]).astype(BF16)


def _dsa_prompt(rel_bias, q, iq, iw, k, v, ik2, nb, l):
    nq = l // BLK
    topk = min(TOPK_MAX, l // 4)
    bkt = jnp.asarray(_bucket_tiles(BLK, BLK))
    assert int(_t5_bucket_np(np.arange(BLK + 1, 4 * BLK)).min()) == REL_BUCKETS - 1
    tok = lambda b, i: (b * nq + i, 0)
    seq = lambda b, i: (b, 0)
    return pl.pallas_call(
        functools.partial(_dsa_prompt_kernel, topk=topk), grid=(nb, nq),
        in_specs=[pl.BlockSpec(memory_space=pltpu.SMEM),
                  pl.BlockSpec((BLK, HD_DSA), tok), pl.BlockSpec((BLK, 1024), tok), pl.BlockSpec((BLK, BLK), tok),
                  pl.BlockSpec((l, HD_DSA), seq), pl.BlockSpec((l, HD_DSA), seq), pl.BlockSpec((l, BLK), seq),
                  pl.BlockSpec((2, BLK, BLK), lambda b, i: (0, 0, 0))],
        out_specs=pl.BlockSpec((BLK, HD_DSA), tok),
        out_shape=jax.ShapeDtypeStruct((nb * l, HD_DSA), BF16),
        scratch_shapes=[pltpu.VMEM((nq, BLK, BLK), I32), pltpu.VMEM((IDX_HEADS, BLK, BLK), F32),
                        pltpu.VMEM((IDX_HEADS * BLK, BLK), BF16), pltpu.VMEM((2 * DSA_HEADS, BLK, BLK), F32),
                        pltpu.VMEM((DSA_HEADS, BLK, BLK), F32), pltpu.VMEM((DSA_HEADS, BLK, BLK), F32),
                        pltpu.VMEM((BLK, HD_DSA), F32)],
        compiler_params=_cparams(("arbitrary", "arbitrary")), name="dsa_prompt",
    )(rel_bias, q, iq, iw, k, v, ik2, bkt)


def _idx_decode_kernel(pt_ref, iq_ref, iw_ref, kp_ref, s_ref, qa_ref, wib_ref):
    p = pl.program_id(1)

    @pl.when(p == 0)
    def _():
        iq = iq_ref[0].astype(F32)
        iw = iw_ref[0]
        for h in range(IDX_HEADS):
            grp = iq[:, (h // 2) * BLK:(h // 2 + 1) * BLK]
            if h % 2 == 1:
                grp = pltpu.roll(grp, IDX_DIM, 1)
            qa_ref[h * SUB:(h + 1) * SUB, :] = grp[:, 0:IDX_DIM]
            wib_ref[h * SUB:(h + 1) * SUB, :] = jnp.broadcast_to(iw[:, h:h + 1] * (IDX_HEADS ** -0.5), (SUB, BLK))

    logits = _dot(qa_ref[...].astype(BF16), kp_ref[...].astype(BF16)) * (IDX_DIM ** -0.5)
    weighted = jnp.maximum(logits, 0.0) * wib_ref[...]
    score = jnp.zeros((SUB, BLK), F32)
    for h in range(IDX_HEADS):
        score = score + weighted[h * SUB:(h + 1) * SUB, :]
    s_ref[0, 0] = score


def _idx_decode(page_table, iq, iw, cache_ik, layer):
    nb, npg = page_table.shape
    seq = lambda b, p, pt: (b, 0, 0)
    grid_spec = pltpu.PrefetchScalarGridSpec(
        num_scalar_prefetch=1, grid=(nb, npg),
        in_specs=[pl.BlockSpec((1, SUB, 1024), seq), pl.BlockSpec((1, SUB, BLK), seq),
                  pl.BlockSpec((None, None, IDX_DIM, BLK), lambda b, p, pt: (layer, pt[b * npg + p], 0, 0))],
        out_specs=pl.BlockSpec((1, 1, SUB, BLK), lambda b, p, pt: (b, p, 0, 0)),
        scratch_shapes=[pltpu.VMEM((IDX_HEADS * SUB, IDX_DIM), F32), pltpu.VMEM((IDX_HEADS * SUB, BLK), F32)])
    return pl.pallas_call(
        _idx_decode_kernel, grid_spec=grid_spec,
        out_shape=jax.ShapeDtypeStruct((nb, npg, SUB, BLK), F32),
        compiler_params=_cparams(("parallel", "arbitrary")), name="idx_decode",
    )(page_table.reshape(-1), iq, iw, cache_ik)


def _dsa_decode_kernel(pt_ref, rb_ref, q_ref, iq_ref, iw_ref, ikn_ref, kn_ref, vn_ref, sc_ref, bkt_ref,
                       kp_ref, vp_ref, o_ref,
                       skey_ref, thr_ref, bias_ref, m_ref, l_ref, acc_ref, kpad_ref, vpad_ref, ikpad_ref,
                       *, topk, npg):
    p = pl.program_id(1)
    lane = lax.broadcasted_iota(I32, (1, BLK), 1)

    def attend(k_pages, v_pages, sel, tile):
        for h in range(DSA_HEADS):
            hs = slice(h * HEAD_DIM, (h + 1) * HEAD_DIM)
            pv, alpha, m_new, l_new = _softmax_block(q_ref[0, :, hs], k_pages[h].astype(BF16),
                                                     v_pages[h].astype(BF16), tile(h), sel, m_ref[h], l_ref[h])
            acc_ref[:, hs] = alpha * acc_ref[:, hs] + pv
            m_ref[h] = m_new
            l_ref[h] = l_new

    @pl.when(p == 0)
    def _():
        kpad_ref[...] = jnp.zeros_like(kpad_ref)
        vpad_ref[...] = jnp.zeros_like(vpad_ref)
        ikpad_ref[...] = jnp.zeros_like(ikpad_ref)
        kpad_ref[:, 0:SUB, :] = kn_ref[0]
        vpad_ref[:, 0:SUB, :] = vn_ref[0]
        ikpad_ref[0:SUB, :] = ikn_ref[0].astype(F32)
        iq = iq_ref[0]
        iw = iw_ref[0]
        ik = ikpad_ref[...].astype(BF16)
        score = jnp.zeros((SUB, BLK), F32)
        for h in range(IDX_HEADS):
            grp = iq[:, (h // 2) * BLK:(h // 2 + 1) * BLK]
            half = (lane < IDX_DIM) if h % 2 == 0 else (lane >= IDX_DIM)
            logits = _dot_nt(jnp.where(half, grp, jnp.zeros_like(grp)), ik) * (IDX_DIM ** -0.5)
            score = score + jnp.maximum(logits, 0.0) * (iw[:, h:h + 1] * (IDX_HEADS ** -0.5))
        row = lax.broadcasted_iota(I32, (SUB, BLK), 0)
        col = lax.broadcasted_iota(I32, (SUB, BLK), 1)
        causal = col <= row
        skey_ref[npg] = jnp.where(causal, _score_key(score), INT_MIN)
        skey_ref[0:npg] = _score_key(sc_ref[0])

        def count_ge(cand):
            hit = jnp.where(skey_ref[...] >= cand[None], 1.0, 0.0)
            return jnp.sum(jnp.sum(hit, axis=0), axis=-1, keepdims=True)

        thr = jnp.broadcast_to(_rank_threshold(count_ge, float(topk), (SUB, 1)), (SUB, BLK))
        thr_ref[...] = thr
        for h in range(DSA_HEADS):
            for t in range(2):
                bias_ref[2 * h + t] = _bias_tile(bkt_ref[t], rb_ref, h)
        m_ref[...] = jnp.full_like(m_ref, NEG_BIG)
        l_ref[...] = jnp.zeros_like(l_ref)
        acc_ref[...] = jnp.zeros_like(acc_ref)
        attend(kpad_ref, vpad_ref, (skey_ref[npg] >= thr) & causal, lambda h: bias_ref[2 * h])

    last = p == npg - 1
    attend(kp_ref, vp_ref, skey_ref[p] >= thr_ref[...],
           lambda h: jnp.where(last, bias_ref[2 * h + 1], rb_ref[REL_BUCKETS - 1, h]))

    @pl.when(last)
    def _():
        for h in range(DSA_HEADS):
            hs = slice(h * HEAD_DIM, (h + 1) * HEAD_DIM)
            o_ref[0, :, hs] = (acc_ref[:, hs] / l_ref[h]).astype(BF16)


def _dsa_decode(page_table, rel_bias, q, iq, iw, ik2n, kn, vn, scores, cache_k, cache_v, layer, n_new):
    nb, npg = page_table.shape
    topk = min(TOPK_MAX, (npg * BLK + n_new) // 4)
    bkt = jnp.asarray(_bucket_tiles(SUB, BLK))
    assert int(_t5_bucket_np(np.arange(BLK + 1, (npg + 1) * BLK + SUB)).min()) == REL_BUCKETS - 1
    seq = lambda b, p, pt: (b, 0, 0)
    seq4 = lambda b, p, pt: (b, 0, 0, 0)
    page = lambda b, p, pt: (layer, pt[b * npg + p], 0, 0, 0)
    page_blk = (None, None, DSA_HEADS, BLK, HEAD_DIM)
    grid_spec = pltpu.PrefetchScalarGridSpec(
        num_scalar_prefetch=1, grid=(nb, npg),
        in_specs=[pl.BlockSpec(memory_space=pltpu.SMEM),
                  pl.BlockSpec((1, SUB, HD_DSA), seq), pl.BlockSpec((1, SUB, 1024), seq),
                  pl.BlockSpec((1, SUB, BLK), seq), pl.BlockSpec((1, SUB, BLK), seq),
                  pl.BlockSpec((1, DSA_HEADS, SUB, HEAD_DIM), seq4), pl.BlockSpec((1, DSA_HEADS, SUB, HEAD_DIM), seq4),
                  pl.BlockSpec((1, npg, SUB, BLK), seq4),
                  pl.BlockSpec((2, SUB, BLK), lambda b, p, pt: (0, 0, 0)),
                  pl.BlockSpec(page_blk, page), pl.BlockSpec(page_blk, page)],
        out_specs=pl.BlockSpec((1, SUB, HD_DSA), seq),
        scratch_shapes=[pltpu.VMEM((npg + 1, SUB, BLK), I32), pltpu.VMEM((SUB, BLK), I32),
                        pltpu.VMEM((2 * DSA_HEADS, SUB, BLK), F32),
                        pltpu.VMEM((DSA_HEADS, SUB, BLK), F32), pltpu.VMEM((DSA_HEADS, SUB, BLK), F32),
                        pltpu.VMEM((SUB, HD_DSA), F32),
                        pltpu.VMEM((DSA_HEADS, BLK, HEAD_DIM), F32), pltpu.VMEM((DSA_HEADS, BLK, HEAD_DIM), F32),
                        pltpu.VMEM((BLK, BLK), F32)])
    return pl.pallas_call(
        functools.partial(_dsa_decode_kernel, topk=topk, npg=npg), grid_spec=grid_spec,
        out_shape=jax.ShapeDtypeStruct((nb, SUB, HD_DSA), BF16),
        compiler_params=_cparams(("parallel", "arbitrary")), name="dsa_decode",
    )(page_table.reshape(-1), rel_bias, q, iq, iw, ik2n, kn, vn, scores, bkt, cache_k, cache_v)


def _out_proj_kernel(x_ref, a_ref, b_ref, c_ref, w_ref, g_ref, y_ref, h_ref):
    y = x_ref[...] + _dot(a_ref[...], w_ref[0:HD_SB, :]) + _dot(b_ref[...], w_ref[HD_SB:HD_SB + 512, :]) \
        + _dot(c_ref[...], w_ref[HD_SB + 512:, :])
    y_ref[...] = y
    h_ref[...] = (_rms_rows(y) * g_ref[...]).astype(BF16)


def _out_proj(x, o_sb, o_ret, o_dsa, w_o, g_ffn, tm):
    t, d = x.shape
    row = lambda i: (i, 0)
    const = lambda i: (0, 0)
    return pl.pallas_call(
        _out_proj_kernel, grid=(t // tm,),
        in_specs=[pl.BlockSpec((tm, d), row), pl.BlockSpec((tm, HD_SB), row), pl.BlockSpec((tm, 512), row),
                  pl.BlockSpec((tm, HD_DSA), row), pl.BlockSpec((d, d), const, pipeline_mode=pl.Buffered(1)),
                  pl.BlockSpec((1, d), const)],
        out_specs=[pl.BlockSpec((tm, d), row), pl.BlockSpec((tm, d), row)],
        out_shape=[jax.ShapeDtypeStruct((t, d), F32), jax.ShapeDtypeStruct((t, d), BF16)],
        compiler_params=_cparams(("parallel",)), name="out_proj",
    )(x, o_sb, o_ret, o_dsa, w_o, g_ffn.reshape(1, -1))


def _ffn_kernel(x_ref, h_ref, wg_ref, wv_ref, wd_ref, cw_ref, cb_ref, pre_ref, y_ref, gate_ref,
                acc_ref, carry_ref, *, seq_rows, tiles_per_seq):
    i = pl.program_id(0)
    f = pl.program_id(1)
    tm = h_ref.shape[0]
    h = h_ref[...]
    gate = _dot(h, wg_ref[...])
    val = _dot(h, wv_ref[...])
    gate_ref[...] = gate[tm - gate_ref.shape[0]:tm, :]
    rid = lax.broadcasted_iota(I32, gate.shape, 0)
    g1 = pltpu.roll(gate, 1, 0)
    g2 = pltpu.roll(gate, 2, 0)
    if seq_rows is None:
        fs = pl.ds(pl.multiple_of(f * FFN_TILE, FFN_TILE), FFN_TILE)
        first = (i % tiles_per_seq) == 0
        prev = jnp.where(first, 0.0, carry_ref[:, fs])
        p1 = jnp.broadcast_to(prev[SUB - 1:SUB, :], gate.shape)
        p2 = jnp.broadcast_to(prev[SUB - 2:SUB - 1, :], gate.shape)
        g1 = jnp.where(rid == 0, p1, g1)
        g2 = jnp.where(rid == 0, p2, jnp.where(rid == 1, p1, g2))
        carry_ref[:, fs] = gate[tm - SUB:tm, :]
    else:
        pre = pre_ref[...]
        pos = rid % seq_rows
        g1 = jnp.where(pos == 0, pltpu.roll(pre, tm - 1, 0), g1)
        g2 = jnp.where(pos < 2, pre, g2)
    conv = cb_ref[...] + cw_ref[0:1, :] * g2 + cw_ref[1:2, :] * g1 + cw_ref[2:3, :] * gate
    act = (conv / (1.0 + jnp.exp(-conv)) * val).astype(BF16)
    part = _dot(act, wd_ref[...])

    @pl.when(f == 0)
    def _():
        acc_ref[...] = x_ref[...] + part

    @pl.when(f > 0)
    def _():
        acc_ref[...] += part

    @pl.when(f == pl.num_programs(1) - 1)
    def _():
        y_ref[...] = acc_ref[...]


def _ffn(x, h, wg, wv, wd, cw, cb, pre, tm, seq_rows, tiles_per_seq):
    t, d = x.shape
    gate_rows = SUB if seq_rows is None else tm
    nf = FFN_PAD // FFN_TILE
    row = lambda i, f: (i, 0)
    ftile = lambda i, f: (0, f)
    return pl.pallas_call(
        functools.partial(_ffn_kernel, seq_rows=seq_rows, tiles_per_seq=tiles_per_seq), grid=(t // tm, nf),
        in_specs=[pl.BlockSpec((tm, d), row), pl.BlockSpec((tm, d), row),
                  pl.BlockSpec((d, FFN_TILE), ftile), pl.BlockSpec((d, FFN_TILE), ftile),
                  pl.BlockSpec((FFN_TILE, d), lambda i, f: (f, 0)),
                  pl.BlockSpec((3, FFN_TILE), ftile), pl.BlockSpec((1, FFN_TILE), ftile),
                  pl.BlockSpec((pre.shape[0], FFN_TILE), ftile)],
        out_specs=[pl.BlockSpec((tm, d), row), pl.BlockSpec((gate_rows, FFN_TILE), lambda i, f: (i, f))],
        out_shape=[jax.ShapeDtypeStruct((t, d), F32), jax.ShapeDtypeStruct((t // tm * gate_rows, FFN_PAD), F32)],
        scratch_shapes=[pltpu.VMEM((tm, d), F32), pltpu.VMEM((SUB, FFN_PAD), F32)],
        compiler_params=_cparams(("arbitrary", "arbitrary")), name="ffn",
    )(x, h, wg, wv, wd, cw, cb, pre)


def kernel(x_prompt, x_sample, cache_sb_k, cache_sb_v, cache_dsa_k, cache_dsa_v, cache_idx_k, state_ret, state_conv,
           page_table, w_in, w_o, g_attn, g_ffn, g_q, g_k, rel_bias, w_up, conv_w, conv_b, w_down):
    nb, l, d = x_prompt.shape
    db, ds, _ = x_sample.shape
    depth = w_in.shape[0]
    n_pool, page = cache_sb_k.shape[1], cache_sb_k.shape[2]
    npg = page_table.shape[1]
    assert d == D_MODEL and page == BLK and l % 512 == 0 and ds <= SUB - 2
    past_len = npg * page
    tp, ts = nb * l, db * SUB

    heads_first = lambda c: jnp.transpose(c, (0, 1, 3, 2, 4))
    c_sb_k, c_sb_v = heads_first(cache_sb_k), heads_first(cache_sb_v)
    c_dsa_k, c_dsa_v = heads_first(cache_dsa_k), heads_first(cache_dsa_v)
    c_idx_k = jnp.transpose(cache_idx_k, (0, 1, 3, 2))
    tokens_first = lambda a: jnp.transpose(a, (0, 2, 1, 3))

    yp = x_prompt.reshape(tp, d)
    ys = jnp.pad(x_sample, ((0, 0), (0, SUB - ds), (0, 0))).reshape(ts, d)
    pos_p = np.arange(l)
    pos_s = past_len + np.arange(SUB)
    zero_state = jnp.zeros((nb, RET_HEADS, RET_QK, HEAD_DIM), F32)
    zero_pre = jnp.zeros((SUB, FFN_PAD), F32)

    outs_p = [[] for _ in range(7)]
    outs_s = [[] for _ in range(7)]
    for layer in range(depth):
        w_sb, w_ret, w_dsa = _prep_in_weights(w_in[layer])
        wo = w_o[layer].astype(BF16)
        wg = _pad_cols(w_up[layer][:, :FFN_DIM], FFN_PAD).astype(BF16)
        wv = _pad_cols(w_up[layer][:, FFN_DIM:], FFN_PAD).astype(BF16)
        wd = jnp.pad(w_down[layer], ((0, FFN_PAD - FFN_DIM), (0, 0))).astype(BF16)
        cw = _pad_cols(conv_w[layer], FFN_PAD)
        cb = _pad_cols(conv_b[layer].reshape(1, -1), FFN_PAD)

        sb, ret, dsa = _in_projections(yp, g_attn[layer], w_sb, w_ret, w_dsa, g_q[layer], g_k[layer], 512, l)
        o_sb = _sb_prompt(sb[0], sb[2], sb[4], nb, l)
        o_ret, s_ret = _retention(ret[0], ret[1], ret[2], ret[3], zero_state, pos_p, BLK, BLK, nb)
        o_dsa = _dsa_prompt(rel_bias, dsa[0], dsa[5], dsa[8], dsa[2], dsa[4], dsa[7], nb, l)
        yp, hp = _out_proj(yp, o_sb, o_ret, o_dsa, wo, g_ffn[layer], 512)
        yp, gate_p = _ffn(yp, hp, wg, wv, wd, cw, cb, zero_pre, 512, None, l // 512)
        for lst, val in zip(outs_p, (tokens_first(sb[1]), tokens_first(sb[3]), tokens_first(dsa[1]),
                                     tokens_first(dsa[3]), dsa[6].reshape(nb, l, IDX_DIM), s_ret,
                                     gate_p.reshape(nb, l // 512, SUB, FFN_PAD)[:, -1, SUB - 2:, :FFN_DIM])):
            lst.append(val)

        sb, ret, dsa = _in_projections(ys, g_attn[layer], w_sb, w_ret, w_dsa, g_q[layer], g_k[layer], ts, SUB)
        r3 = lambda a: a.reshape(db, SUB, a.shape[-1])
        o_sb = _sb_decode(page_table, r3(sb[0]), sb[1], sb[3], c_sb_k, c_sb_v, layer)
        o_ret, s_ret = _retention(ret[0], ret[1], ret[2], ret[3], state_ret[layer], pos_s, SUB, ds, db)
        scores = _idx_decode(page_table, r3(dsa[5]), r3(dsa[8]), c_idx_k, layer)
        o_dsa = _dsa_decode(page_table, rel_bias, r3(dsa[0]), r3(dsa[5]), r3(dsa[8]), r3(dsa[7]), dsa[1],
                            dsa[3], scores, c_dsa_k, c_dsa_v, layer, ds)
        ys, hs = _out_proj(ys, o_sb.reshape(ts, HD_SB), o_ret, o_dsa.reshape(ts, HD_DSA), wo, g_ffn[layer], ts)
        pre = jnp.pad(state_conv[layer], ((0, 0), (0, SUB - 2), (0, FFN_PAD - FFN_DIM))).reshape(ts, FFN_PAD)
        ys, gate_s = _ffn(ys, hs, wg, wv, wd, cw, cb, pre, ts, SUB, 1)
        cut = lambda a: tokens_first(a)[:, :ds]
        for lst, val in zip(outs_s, (cut(sb[1]), cut(sb[3]), cut(dsa[1]), cut(dsa[3]),
                                     dsa[6].reshape(db, SUB, IDX_DIM)[:, :ds], s_ret,
                                     gate_s.reshape(db, SUB, FFN_PAD)[:, ds - 2:ds, :FFN_DIM])):
            lst.append(val)

    stk = lambda lst: jnp.stack(lst)
    return ((yp.reshape(nb, l, d), ys.reshape(db, SUB, d)[:, :ds])
            + tuple(stk(o) for o in outs_p) + tuple(stk(o) for o in outs_s))
```

```python
import functools
import math

import numpy as np
import jax
import jax.numpy as jnp
from jax import lax
from jax.experimental import pallas as pl
from jax.experimental.pallas import tpu as pltpu

F32 = jnp.float32
BF16 = jnp.bfloat16
I32 = jnp.int32

D_MODEL = 2048
HEAD_DIM = 128
SB_HEADS = 6
RET_HEADS = 4
DSA_HEADS = 6
RET_QK = 64
RET_HALF = RET_QK // 2
IDX_HEADS = 16
IDX_DIM = 64
TOPK_MAX = 256
REL_BUCKETS = 32
REL_MAX_DIST = 128
FFN_DIM = 5504
FFN_PAD = 5632
FFN_TILE = 512
RMS_EPS = 1e-6
ROPE_BASE = 10000.0
BLK = 128
SUB = 8
W_SB = 3 * SB_HEADS * HEAD_DIM
W_RET = 2 * RET_HEADS * RET_QK + 2 * RET_HEADS * HEAD_DIM
W_DSA = 3 * DSA_HEADS * HEAD_DIM + IDX_HEADS * IDX_DIM + 2 * BLK
HD_SB = SB_HEADS * HEAD_DIM
HD_DSA = DSA_HEADS * HEAD_DIM
INT_MIN = np.int32(-2 ** 31)
NEG_BIG = -1e30
VMEM_LIMIT = 56 * 1024 * 1024

_NT = (((1,), (1,)), ((), ()))
_TN = (((0,), (0,)), ((), ()))


def _cparams(sem):
    return pltpu.CompilerParams(dimension_semantics=sem, vmem_limit_bytes=VMEM_LIMIT)


def _dot(a, b):
    return jnp.dot(a, b, preferred_element_type=F32)


def _dot_nt(a, b):
    return lax.dot_general(a, b, _NT, preferred_element_type=F32)


def _rms_rows(x):
    return x * lax.rsqrt(jnp.mean(x * x, axis=-1, keepdims=True) + RMS_EPS)


def _t5_bucket_np(dist):
    max_exact = REL_BUCKETS // 2
    dist = np.maximum(dist, 0)
    far = max_exact + (np.log(np.maximum(dist, 1).astype(np.float32) / np.float32(max_exact))
                       / np.float32(math.log(REL_MAX_DIST / max_exact))
                       * np.float32(REL_BUCKETS - max_exact)).astype(np.int32)
    return np.where(dist < max_exact, dist, np.minimum(far, REL_BUCKETS - 1)).astype(np.int32)


def _bucket_tiles(rows, base):
    r = np.arange(rows)[:, None]
    c = np.arange(BLK)[None, :]
    return np.stack([_t5_bucket_np(r - c), _t5_bucket_np(base + r - c)]).astype(np.int32)


def _rope_tables(pos):
    inv_freq = (1.0 / (np.float32(ROPE_BASE) ** np.linspace(0.0, 1.0, RET_HALF, dtype=np.float32))).astype(np.float32)
    ang = pos.astype(np.float32)[:, None] * inv_freq[None, :]
    cos = np.tile(np.cos(ang).astype(np.float32), (1, RET_HEADS))
    sin = np.tile(np.sin(ang).astype(np.float32), (1, RET_HEADS))
    return cos, sin


def _ret_tables(rows, n):
    lg = np.log(np.float32(1.0) - np.float32(2.0) ** (-5.0 - np.arange(RET_HEADS, dtype=np.float32))).astype(np.float32)
    i = np.arange(BLK, dtype=np.float32)
    diff = i[:, None] - i[None, :]
    valid = (np.arange(BLK) < n)
    dec = np.where(diff >= 0, np.exp(lg[:, None, None] * np.maximum(diff, 0.0)), 0.0)
    dec = (dec * valid[None, :, None] * valid[None, None, :])[:, :rows, :]
    qd = np.exp(lg[:, None] * (i + 1.0)[None, :]) * valid[None, :]
    kd = np.exp(lg[:, None] * np.maximum(n - 1.0 - i, 0.0)[None, :]) * valid[None, :]
    gn = np.exp(lg * np.float32(n))
    qd = np.broadcast_to(qd[:, :rows, None], (RET_HEADS, rows, BLK))
    kd = np.broadcast_to(kd[:, :, None], (RET_HEADS, BLK, BLK))
    return (dec.astype(np.float32), np.ascontiguousarray(qd, dtype=np.float32),
            np.ascontiguousarray(kd, dtype=np.float32), [float(g) for g in gn])


def _cumsum_matrix():
    r = np.arange(2 * BLK)[:, None] % BLK
    c = np.arange(2 * BLK)[None, :]
    return np.where(c < BLK, r > c, True).astype(np.float32)


def _prep_in_weights(w):
    perm = np.array([h * RET_QK + a * RET_HALF + i for a in range(2) for h in range(RET_HEADS) for i in range(RET_HALF)])
    w_sb = w[:, :2304]
    w_ret = jnp.concatenate([w[:, 2304:2560][:, perm], w[:, 2560:2816][:, perm], w[:, 2816:3840]], axis=1)
    ik = w[:, 7184:7248]
    w_dsa = jnp.concatenate([w[:, 3840:7168], ik, ik, w[:, 7168:7184],
                             jnp.zeros((w.shape[0], BLK - IDX_HEADS), w.dtype)], axis=1)
    return w_sb.astype(BF16), w_ret.astype(BF16), w_dsa.astype(BF16)


def _pad_cols(a, n):
    return jnp.pad(a, ((0, 0), (0, n - a.shape[1])))


def _norm_in(x_ref, g_ref):
    x = x_ref[...]
    return (_rms_rows(x) * g_ref[...]).astype(BF16)


def _store_heads(ref, a):
    tb, nh, lt, _ = ref.shape
    for h in range(nh):
        ref[:, h] = a[:, h * HEAD_DIM:(h + 1) * HEAD_DIM].reshape(tb, lt, HEAD_DIM)


def _proj_sb_kernel(x_ref, g_ref, w_ref, q_ref, k_ref, kb_ref, v_ref, vb_ref):
    h = _norm_in(x_ref, g_ref)
    q_ref[...] = _dot(h, w_ref[:, 0:HD_SB]).astype(BF16)
    k = _dot(h, w_ref[:, HD_SB:2 * HD_SB])
    _store_heads(k_ref, k)
    kb_ref[...] = k.astype(BF16)
    v = _dot(h, w_ref[:, 2 * HD_SB:3 * HD_SB])
    _store_heads(v_ref, v)
    vb_ref[...] = v.astype(BF16)


def _proj_ret_kernel(x_ref, g_ref, w_ref, q_ref, k_ref, v_ref, gate_ref):
    h = _norm_in(x_ref, g_ref)
    q_ref[...] = _dot(h, w_ref[:, 0:256])
    k_ref[...] = _dot(h, w_ref[:, 256:512])
    v_ref[...] = _dot(h, w_ref[:, 512:1024]).astype(BF16)
    gate_ref[...] = _dot(h, w_ref[:, 1024:1536])


def _proj_dsa_kernel(x_ref, g_ref, w_ref, gq_ref, gk_ref, q_ref, k_ref, kb_ref, v_ref, vb_ref,
                     iq_ref, ik_ref, ik2_ref, iw_ref):
    h = _norm_in(x_ref, g_ref)
    tb, _, lt, _ = k_ref.shape
    for hh in range(DSA_HEADS):
        lo, hi = hh * HEAD_DIM, (hh + 1) * HEAD_DIM
        q_ref[:, lo:hi] = (_rms_rows(_dot(h, w_ref[:, lo:hi])) * gq_ref[...]).astype(BF16)
        k = _rms_rows(_dot(h, w_ref[:, HD_DSA + lo:HD_DSA + hi])) * gk_ref[...]
        k_ref[:, hh] = k.reshape(tb, lt, HEAD_DIM)
        kb_ref[:, lo:hi] = k.astype(BF16)
    v = _dot(h, w_ref[:, 2 * HD_DSA:3 * HD_DSA])
    _store_heads(v_ref, v)
    vb_ref[...] = v.astype(BF16)
    c0 = 3 * HD_DSA
    iq_ref[...] = _dot(h, w_ref[:, c0:c0 + 1024]).astype(BF16)
    ik2 = _dot(h, w_ref[:, c0 + 1024:c0 + 1152])
    ik_ref[...] = ik2[:, 0:IDX_DIM]
    ik2_ref[...] = ik2.astype(BF16)
    iw_ref[...] = _dot(h, w_ref[:, c0 + 1152:c0 + 1280])


def _proj_call(kernel, x, g, w, extra, outs, tm, seq_rows, name):
    t, d = x.shape
    row = lambda i: (i, 0)
    const = lambda i: (0, 0)
    if tm >= seq_rows:
        tb, lt, head_idx = tm // seq_rows, seq_rows, (lambda i: (i, 0, 0, 0))
    else:
        tiles = seq_rows // tm
        tb, lt, head_idx = 1, tm, (lambda i: (i // tiles, 0, i % tiles, 0))
    in_specs = [pl.BlockSpec((tm, d), row), pl.BlockSpec((1, d), const),
                pl.BlockSpec(w.shape, const, pipeline_mode=pl.Buffered(1))]
    in_specs += [pl.BlockSpec(e.shape, const) for e in extra]
    out_specs, out_shape = [], []
    for kind, arg in outs:
        if kind == "heads":
            out_specs.append(pl.BlockSpec((tb, arg, lt, HEAD_DIM), head_idx))
            out_shape.append(jax.ShapeDtypeStruct((t // seq_rows, arg, seq_rows, HEAD_DIM), F32))
        else:
            out_specs.append(pl.BlockSpec((tm, kind), row))
            out_shape.append(jax.ShapeDtypeStruct((t, kind), arg))
    return pl.pallas_call(
        kernel, grid=(t // tm,), in_specs=in_specs, out_specs=out_specs, out_shape=out_shape,
        compiler_params=_cparams(("parallel",)), name=name,
    )(x, g, w, *extra)


def _in_projections(x, g_attn, w_sb, w_ret, w_dsa, g_q, g_k, tm, seq_rows):
    g = g_attn.reshape(1, -1)
    sb = _proj_call(_proj_sb_kernel, x, g, w_sb, [],
                    [(HD_SB, BF16), ("heads", SB_HEADS), (HD_SB, BF16), ("heads", SB_HEADS), (HD_SB, BF16)],
                    tm, seq_rows, "proj_sb")
    ret = _proj_call(_proj_ret_kernel, x, g, w_ret, [],
                     [(256, F32), (256, F32), (512, BF16), (512, F32)], tm, seq_rows, "proj_ret")
    dsa = _proj_call(_proj_dsa_kernel, x, g, w_dsa, [g_q.reshape(1, -1), g_k.reshape(1, -1)],
                     [(HD_DSA, BF16), ("heads", DSA_HEADS), (HD_DSA, BF16), ("heads", DSA_HEADS), (HD_DSA, BF16),
                      (1024, BF16), (IDX_DIM, F32), (BLK, BF16), (BLK, F32)], tm, seq_rows, "proj_dsa")
    return sb, ret, dsa


def _sb_phased(qs, ks, vs, u, runs, visibles):
    nh, nj = len(qs), len(visibles)
    z = [[_dot_nt(qs[h], ks[h][j]) * (HEAD_DIM ** -0.5) for j in range(nj)] for h in range(nh)]
    sp = [[None] * nj for _ in range(nh)]
    cs = [[None] * nj for _ in range(nh)]
    for h in range(nh):
        for j in range(nj):
            zz = z[h][j]
            s = jnp.maximum(zz, 0.0) + jnp.log(1.0 + jnp.exp(-jnp.abs(zz)))
            lk = -s if visibles[j] is None else jnp.where(visibles[j], -s, 0.0)
            hi = lk.astype(BF16)
            lo = (lk - hi.astype(F32)).astype(BF16)
            sp[h][j] = s
            cs[h][j] = _dot(jnp.concatenate([hi, lo], axis=1), u)
    outs = []
    for h in range(nh):
        run, o = runs[h], None
        for j in range(nj):
            a = jnp.exp(z[h][j] - sp[h][j] + cs[h][j][:, :BLK] + run)
            if visibles[j] is not None:
                a = jnp.where(visibles[j], a, 0.0)
            pv = _dot(a.astype(BF16), vs[h][j])
            o = pv if o is None else o + pv
            run = run + cs[h][j][:, BLK:]
        outs.append((o, run))
    return outs


def _head_slices(n):
    return [slice(h * HEAD_DIM, (h + 1) * HEAD_DIM) for h in range(n)]


def _sb_prompt_kernel(q_ref, k_ref, v_ref, u_ref, o_ref, acc_ref, run_ref):
    qi = pl.program_id(1)
    u = u_ref[...]
    acc_ref[...] = jnp.zeros_like(acc_ref)
    run_ref[...] = jnp.zeros_like(run_ref)
    row = lax.broadcasted_iota(I32, (BLK, BLK), 0)
    col = lax.broadcasted_iota(I32, (BLK, BLK), 1)
    strict = col < row
    heads = _head_slices(SB_HEADS)

    def step(kbs, visibles):
        offs = [pl.multiple_of(kb * BLK, BLK) for kb in kbs]
        outs = _sb_phased([q_ref[:, hs] for hs in heads],
                          [[k_ref[pl.ds(off, BLK), hs] for off in offs] for hs in heads],
                          [[v_ref[pl.ds(off, BLK), hs] for off in offs] for hs in heads],
                          u, [run_ref[:, hs] for hs in heads], visibles)
        for hs, (o, run) in zip(heads, outs):
            acc_ref[:, hs] += o
            run_ref[:, hs] = run

    odd = qi % 2

    @pl.when(odd == 1)
    def _():
        step([qi, qi - 1], [strict, None])

    @pl.when(odd == 0)
    def _():
        step([qi], [strict])

    start = qi - 1 - odd

    def body(t, carry):
        step([start - 2 * t, start - 2 * t - 1], [None, None])
        return carry

    lax.fori_loop(0, (start + 1) // 2, body, 0)
    o_ref[...] = acc_ref[...].astype(BF16)


def _sb_prompt(q, k, v, nb, l):
    u = jnp.asarray(_cumsum_matrix(), BF16)
    nq = l // BLK
    return pl.pallas_call(
        _sb_prompt_kernel, grid=(nb, nq),
        in_specs=[pl.BlockSpec((BLK, HD_SB), lambda b, i: (b * nq + i, 0)),
                  pl.BlockSpec((l, HD_SB), lambda b, i: (b, 0)),
                  pl.BlockSpec((l, HD_SB), lambda b, i: (b, 0)),
                  pl.BlockSpec((2 * BLK, 2 * BLK), lambda b, i: (0, 0))],
        out_specs=pl.BlockSpec((BLK, HD_SB), lambda b, i: (b * nq + i, 0)),
        out_shape=jax.ShapeDtypeStruct((nb * l, HD_SB), BF16),
        scratch_shapes=[pltpu.VMEM((BLK, HD_SB), F32), pltpu.VMEM((BLK, HD_SB), F32)],
        compiler_params=_cparams(("parallel", "arbitrary")), name="sb_prompt",
    )(q, k, v, u)


def _pages_per_step(npg):
    return math.gcd(npg, 4)


def _sb_decode_kernel(pt_ref, q_ref, kn_ref, vn_ref, u_ref, *rest, pg):
    kp_refs, vp_refs = rest[:pg], rest[pg:2 * pg]
    o_ref, acc_ref, run_ref, kpad_ref, vpad_ref = rest[2 * pg:]
    p = pl.program_id(1)
    u = u_ref[...]
    heads = _head_slices(SB_HEADS)

    def step(k_pages, v_pages, visibles):
        outs = _sb_phased([q_ref[0, :, hs] for hs in heads],
                          [[kp[h].astype(BF16) for kp in k_pages] for h in range(SB_HEADS)],
                          [[vp[h].astype(BF16) for vp in v_pages] for h in range(SB_HEADS)],
                          u, [run_ref[:, hs] for hs in heads], visibles)
        for hs, (o, run) in zip(heads, outs):
            acc_ref[:, hs] += o
            run_ref[:, hs] = run

    @pl.when(p == 0)
    def _():
        acc_ref[...] = jnp.zeros_like(acc_ref)
        run_ref[...] = jnp.zeros_like(run_ref)
        kpad_ref[...] = jnp.zeros_like(kpad_ref)
        vpad_ref[...] = jnp.zeros_like(vpad_ref)
        kpad_ref[:, 0:SUB, :] = kn_ref[0]
        vpad_ref[:, 0:SUB, :] = vn_ref[0]
        row = lax.broadcasted_iota(I32, (SUB, BLK), 0)
        col = lax.broadcasted_iota(I32, (SUB, BLK), 1)
        step([kpad_ref], [vpad_ref], [col < row])

    step(kp_refs, vp_refs, [None] * pg)

    @pl.when(p == pl.num_programs(1) - 1)
    def _():
        o_ref[0] = acc_ref[...].astype(BF16)


def _sb_decode(page_table, q, kn, vn, cache_k, cache_v, layer):
    nb, npg = page_table.shape
    pg = _pages_per_step(npg)
    u = jnp.asarray(_cumsum_matrix(), BF16)
    seq = lambda b, p, pt: (b, 0, 0)
    seq4 = lambda b, p, pt: (b, 0, 0, 0)
    page_blk = (None, None, SB_HEADS, BLK, HEAD_DIM)

    def page(j):
        return lambda b, p, pt: (layer, pt[b * npg + (npg - 1 - (p * pg + j))], 0, 0, 0)

    pages = [pl.BlockSpec(page_blk, page(j)) for j in range(pg)]
    grid_spec = pltpu.PrefetchScalarGridSpec(
        num_scalar_prefetch=1, grid=(nb, npg // pg),
        in_specs=[pl.BlockSpec((1, SUB, HD_SB), seq), pl.BlockSpec((1, SB_HEADS, SUB, HEAD_DIM), seq4),
                  pl.BlockSpec((1, SB_HEADS, SUB, HEAD_DIM), seq4),
                  pl.BlockSpec((2 * BLK, 2 * BLK), lambda b, p, pt: (0, 0))] + pages + pages,
        out_specs=pl.BlockSpec((1, SUB, HD_SB), seq),
        scratch_shapes=[pltpu.VMEM((SUB, HD_SB), F32), pltpu.VMEM((SUB, HD_SB), F32),
                        pltpu.VMEM((SB_HEADS, BLK, HEAD_DIM), F32), pltpu.VMEM((SB_HEADS, BLK, HEAD_DIM), F32)])
    return pl.pallas_call(
        functools.partial(_sb_decode_kernel, pg=pg), grid_spec=grid_spec,
        out_shape=jax.ShapeDtypeStruct((nb, SUB, HD_SB), BF16),
        compiler_params=_cparams(("parallel", "arbitrary")), name="sb_decode",
    )(page_table.reshape(-1), q, kn, vn, u, *([cache_k] * pg), *([cache_v] * pg))


def _ret_kernel(q_ref, k_ref, v_ref, g_ref, cos_ref, sin_ref, dec_ref, qd_ref, kd_ref, s0_ref,
                o_ref, sout_ref, s_ref, kc_ref, vp_ref, *, gn):
    c = pl.program_id(1)
    rows = q_ref.shape[0]

    @pl.when(c == 0)
    def _():
        s_ref[...] = jnp.zeros_like(s_ref)
        kc_ref[...] = jnp.zeros_like(kc_ref)
        vp_ref[...] = jnp.zeros_like(vp_ref)
        for h in range(RET_HEADS):
            s_ref[h, h * RET_HALF:(h + 1) * RET_HALF, :] = s0_ref[0, h, 0:RET_HALF, :]
            s_ref[h, BLK + h * RET_HALF:BLK + (h + 1) * RET_HALF, :] = s0_ref[0, h, RET_HALF:RET_QK, :]

    cos = cos_ref[...]
    sin = sin_ref[...]
    q = q_ref[...]
    k = k_ref[...]
    q1 = q[:, :BLK] * cos - q[:, BLK:] * sin
    q2 = q[:, :BLK] * sin + q[:, BLK:] * cos
    k1 = (k[:, :BLK] * cos - k[:, BLK:] * sin) * (RET_QK ** -0.5)
    k2 = (k[:, :BLK] * sin + k[:, BLK:] * cos) * (RET_QK ** -0.5)
    kc_ref[0:rows, :] = jnp.concatenate([k1, k2], axis=1)
    vp_ref[0:rows, :] = v_ref[...].astype(F32)
    kc = kc_ref[...]
    kcat = kc.astype(BF16)
    lane = lax.broadcasted_iota(I32, (1, BLK), 1)
    for h in range(RET_HEADS):
        hs = slice(h * HEAD_DIM, (h + 1) * HEAD_DIM)
        m = jnp.where((lane >= h * RET_HALF) & (lane < (h + 1) * RET_HALF), 1.0, 0.0)
        qm = jnp.concatenate([q1 * m, q2 * m], axis=1).astype(BF16)
        km = (kc * jnp.concatenate([m, m], axis=1)).astype(BF16)
        v = vp_ref[:, hs]
        att = _dot_nt(qm, kcat) * dec_ref[h]
        s = s_ref[h]
        o = _dot(att.astype(BF16), v.astype(BF16)) + _dot(qm, s.astype(BF16)) * qd_ref[h]
        vd = (v * kd_ref[h]).astype(BF16)
        s_ref[h] = gn[h] * s + lax.dot_general(km, vd, _TN, preferred_element_type=F32)
        gate = g_ref[:, hs]
        o_ref[:, hs] = (_rms_rows(o) * (gate / (1.0 + jnp.exp(-gate)))).astype(BF16)

    @pl.when(c == pl.num_programs(1) - 1)
    def _():
        for h in range(RET_HEADS):
            sout_ref[0, h, 0:RET_HALF, :] = s_ref[h, h * RET_HALF:(h + 1) * RET_HALF, :]
            sout_ref[0, h, RET_HALF:RET_QK, :] = s_ref[h, BLK + h * RET_HALF:BLK + (h + 1) * RET_HALF, :]


def _retention(q, k, v, g, s0, pos, rows, n, nb):
    nc = q.shape[0] // (nb * rows)
    cos, sin = _rope_tables(pos)
    dec, qd, kd, gn = _ret_tables(rows, n)
    tok = lambda b, c: (b * nc + c, 0)
    full3 = lambda b, c: (0, 0, 0)
    return pl.pallas_call(
        functools.partial(_ret_kernel, gn=gn), grid=(nb, nc),
        in_specs=[pl.BlockSpec((rows, 256), tok), pl.BlockSpec((rows, 256), tok),
                  pl.BlockSpec((rows, 512), tok), pl.BlockSpec((rows, 512), tok),
                  pl.BlockSpec((rows, BLK), lambda b, c: (c, 0)), pl.BlockSpec((rows, BLK), lambda b, c: (c, 0)),
                  pl.BlockSpec((RET_HEADS, rows, BLK), full3), pl.BlockSpec((RET_HEADS, rows, BLK), full3),
                  pl.BlockSpec((RET_HEADS, BLK, BLK), full3),
                  pl.BlockSpec((1, RET_HEADS, RET_QK, HEAD_DIM), lambda b, c: (b, 0, 0, 0))],
        out_specs=[pl.BlockSpec((rows, 512), tok),
                   pl.BlockSpec((1, RET_HEADS, RET_QK, HEAD_DIM), lambda b, c: (b, 0, 0, 0))],
        out_shape=[jax.ShapeDtypeStruct((q.shape[0], 512), BF16),
                   jax.ShapeDtypeStruct((nb, RET_HEADS, RET_QK, HEAD_DIM), F32)],
        scratch_shapes=[pltpu.VMEM((RET_HEADS, 2 * BLK, HEAD_DIM), F32), pltpu.VMEM((BLK, 2 * BLK), F32),
                        pltpu.VMEM((BLK, RET_HEADS * HEAD_DIM), F32)],
        compiler_params=_cparams(("parallel", "arbitrary")), name="retention",
    )(q, k, v, g, jnp.asarray(cos), jnp.asarray(sin), jnp.asarray(dec), jnp.asarray(qd), jnp.asarray(kd), s0)


def _score_key(score):
    score = jnp.where(score == 0.0, 0.0, score)
    bits = lax.bitcast_convert_type(score, I32)
    return jnp.where(bits < 0, bits ^ np.int32(0x7FFFFFFF), bits)


def _rank_threshold(count_ge, topk, shape):
    ans = jnp.zeros(shape, I32)
    for bit in range(31, -1, -1):
        cand = ans | np.int32(-2 ** 31 if bit == 31 else 2 ** bit)
        ans = jnp.where(count_ge(cand ^ INT_MIN) >= topk, cand, ans)
    return ans ^ INT_MIN


def _bias_tile(bkt, rb_ref, h):
    t = jnp.zeros(bkt.shape, F32)
    for b in range(REL_BUCKETS):
        t = jnp.where(bkt == b, rb_ref[b, h], t)
    return t


def _softmax_phased(qs, ks, vs, biases, sels, m_olds, l_olds, mxu_rowsum):
    nh, nj = len(qs), len(sels)
    s = [[jnp.where(sels[j], _dot_nt(qs[h], ks[h][j]) * (HEAD_DIM ** -0.5) + biases[h][j], NEG_BIG)
          for j in range(nj)] for h in range(nh)]
    stats, probs = [], []
    for h in range(nh):
        mx = s[h][0]
        for j in range(1, nj):
            mx = jnp.maximum(mx, s[h][j])
        m_new = jnp.maximum(m_olds[h], jnp.max(mx, axis=-1, keepdims=True))
        stats.append((jnp.exp(m_olds[h] - m_new), m_new))
        probs.append([jnp.where(sels[j], jnp.exp(s[h][j] - m_new), 0.0) for j in range(nj)])
    outs = []
    for h in range(nh):
        alpha, m_new = stats[h]
        if mxu_rowsum:
            ones = jnp.ones((BLK, BLK), BF16)
            p_cat = jnp.concatenate([p.astype(BF16) for p in probs[h]], axis=1)
            v_ext = jnp.concatenate([jnp.concatenate([vs[h][j], ones], axis=1) for j in range(nj)], axis=0)
            r = _dot(p_cat, v_ext)
            pv, rs = r[:, :BLK], r[:, BLK:]
        else:
            pv, tot = None, None
            for j in range(nj):
                d = _dot(probs[h][j].astype(BF16), vs[h][j])
                pv = d if pv is None else pv + d
                tot = probs[h][j] if tot is None else tot + probs[h][j]
            rs = jnp.sum(tot, axis=-1, keepdims=True)
        outs.append((pv, alpha, m_new, alpha * l_olds[h] + rs))
    return outs


def _dsa_prompt_kernel(rb_ref, q_ref, iq_ref, iw_ref, k_ref, v_ref, ik_ref, bkt_ref, o_ref,
                       skey_ref, wib_ref, qm_ref, bias_ref, m_ref, l_ref, acc_ref, *, topk):
    qi = pl.program_id(1)
    ca = qi // 2
    row = lax.broadcasted_iota(I32, (BLK, BLK), 0)
    col = lax.broadcasted_iota(I32, (BLK, BLK), 1)
    lane = lax.broadcasted_iota(I32, (1, BLK), 1)
    heads = _head_slices(DSA_HEADS)

    def causal(kb):
        return (col + kb * BLK) <= (row + qi * BLK)

    @pl.when((pl.program_id(0) == 0) & (qi == 0))
    def _():
        for h in range(DSA_HEADS):
            for t in range(2):
                bias_ref[2 * h + t] = _bias_tile(bkt_ref[t], rb_ref, h)

    iw = iw_ref[...]
    for h in range(IDX_HEADS):
        grp = iq_ref[:, (h // 2) * BLK:(h // 2 + 1) * BLK]
        half = (lane < IDX_DIM) if h % 2 == 0 else (lane >= IDX_DIM)
        qm_ref[h * BLK:(h + 1) * BLK, :] = jnp.where(half, grp, jnp.zeros_like(grp))
        wib_ref[h] = jnp.broadcast_to(iw[:, h:h + 1] * (IDX_HEADS ** -0.5 * IDX_DIM ** -0.5), (BLK, 2 * BLK))

    def score_pair(c, diag):
        off = pl.multiple_of(c * 2 * BLK, 2 * BLK)
        logits = _dot_nt(qm_ref[...], ik_ref[pl.ds(off, 2 * BLK), :])
        score = jnp.zeros((BLK, 2 * BLK), F32)
        for h in range(IDX_HEADS):
            score = score + jnp.maximum(logits[h * BLK:(h + 1) * BLK, :], 0.0) * wib_ref[h]
        key = _score_key(score)
        for t in range(2):
            kt = key[:, t * BLK:(t + 1) * BLK]
            if diag:
                kt = jnp.where(causal(2 * c + t), kt, INT_MIN)
            skey_ref[2 * c + t] = kt

    score_pair(ca, True)

    def score_body(c, carry):
        score_pair(c, False)
        return carry

    lax.fori_loop(0, ca, score_body, 0)

    ones = jnp.ones((BLK, BLK), BF16)

    def count_ge(cand):
        def body(c, acc):
            return (acc + jnp.where(skey_ref[2 * c] >= cand, 1.0, 0.0)
                    + jnp.where(skey_ref[2 * c + 1] >= cand, 1.0, 0.0))
        acc = lax.fori_loop(0, ca + 1, body, jnp.zeros((BLK, BLK), F32))
        return _dot(acc.astype(BF16), ones)

    thr = _rank_threshold(count_ge, float(topk), (BLK, BLK))

    m_ref[...] = jnp.full_like(m_ref, NEG_BIG)
    l_ref[...] = jnp.zeros_like(l_ref)
    acc_ref[...] = jnp.zeros_like(acc_ref)

    def attend(c, near):
        kbs = [2 * c, 2 * c + 1]
        offs = [pl.multiple_of(kb * BLK, BLK) for kb in kbs]
        sels = [skey_ref[kb] >= thr for kb in kbs]
        far = [rb_ref[REL_BUCKETS - 1, h] for h in range(DSA_HEADS)]
        if near:
            sels = [sel & causal(kb) for kb, sel in zip(kbs, sels)]
            biases = [[jnp.where(qi - kb == 0, bias_ref[2 * h], jnp.where(qi - kb == 1, bias_ref[2 * h + 1], far[h]))
                       for kb in kbs] for h in range(DSA_HEADS)]
        else:
            biases = [[far[h]] * 2 for h in range(DSA_HEADS)]
        outs = _softmax_phased([q_ref[:, hs] for hs in heads],
                               [[k_ref[pl.ds(off, BLK), hs] for off in offs] for hs in heads],
                               [[v_ref[pl.ds(off, BLK), hs] for off in offs] for hs in heads],
                               biases, sels, [m_ref[h] for h in range(DSA_HEADS)],
                               [l_ref[h] for h in range(DSA_HEADS)], True)
        for h, (pv, alpha, m_new, l_new) in enumerate(outs):
            acc_ref[:, heads[h]] = alpha * acc_ref[:, heads[h]] + pv
            m_ref[h] = m_new
            l_ref[h] = l_new

    attend(ca, True)

    @pl.when(ca >= 1)
    def _():
        attend(ca - 1, True)

    def far_body(c, carry):
        attend(c, False)
        return carry

    lax.fori_loop(0, ca - 1, far_body, 0)

    for h, hs in enumerate(heads):
        o_ref[:, hs] = (acc_ref[:, hs] / l_ref[h]).astype(BF16)


def _dsa_prompt(rel_bias, q, iq, iw, k, v, ik2, nb, l):
    nq = l // BLK
    topk = min(TOPK_MAX, l // 4)
    bkt = jnp.asarray(_bucket_tiles(BLK, BLK))
    assert int(_t5_bucket_np(np.arange(BLK + 1, 4 * BLK)).min()) == REL_BUCKETS - 1
    tok = lambda b, i: (b * nq + i, 0)
    seq = lambda b, i: (b, 0)
    return pl.pallas_call(
        functools.partial(_dsa_prompt_kernel, topk=topk), grid=(nb, nq),
        in_specs=[pl.BlockSpec(memory_space=pltpu.SMEM),
                  pl.BlockSpec((BLK, HD_DSA), tok), pl.BlockSpec((BLK, 1024), tok), pl.BlockSpec((BLK, BLK), tok),
                  pl.BlockSpec((l, HD_DSA), seq), pl.BlockSpec((l, HD_DSA), seq), pl.BlockSpec((l, BLK), seq),
                  pl.BlockSpec((2, BLK, BLK), lambda b, i: (0, 0, 0))],
        out_specs=pl.BlockSpec((BLK, HD_DSA), tok),
        out_shape=jax.ShapeDtypeStruct((nb * l, HD_DSA), BF16),
        scratch_shapes=[pltpu.VMEM((nq, BLK, BLK), I32), pltpu.VMEM((IDX_HEADS, BLK, 2 * BLK), F32),
                        pltpu.VMEM((IDX_HEADS * BLK, BLK), BF16), pltpu.VMEM((2 * DSA_HEADS, BLK, BLK), F32),
                        pltpu.VMEM((DSA_HEADS, BLK, BLK), F32), pltpu.VMEM((DSA_HEADS, BLK, BLK), F32),
                        pltpu.VMEM((BLK, HD_DSA), F32)],
        compiler_params=_cparams(("arbitrary", "arbitrary")), name="dsa_prompt",
    )(rel_bias, q, iq, iw, k, v, ik2, bkt)


def _idx_decode_kernel(pt_ref, iq_ref, iw_ref, *rest, pg):
    kp_refs = rest[:pg]
    s_ref, qa_ref, wib_ref = rest[pg:]
    p = pl.program_id(1)

    @pl.when(p == 0)
    def _():
        iq = iq_ref[0].astype(F32)
        iw = iw_ref[0]
        for h in range(IDX_HEADS):
            grp = iq[:, (h // 2) * BLK:(h // 2 + 1) * BLK]
            if h % 2 == 1:
                grp = pltpu.roll(grp, IDX_DIM, 1)
            qa_ref[h * SUB:(h + 1) * SUB, :] = grp[:, 0:IDX_DIM]
            wib_ref[h * SUB:(h + 1) * SUB, :] = jnp.broadcast_to(
                iw[:, h:h + 1] * (IDX_HEADS ** -0.5 * IDX_DIM ** -0.5), (SUB, BLK))

    qa = qa_ref[...].astype(BF16)
    logits = [_dot(qa, kp[...].astype(BF16)) for kp in kp_refs]
    for j in range(pg):
        weighted = jnp.maximum(logits[j], 0.0) * wib_ref[...]
        score = weighted[0:SUB, :]
        for h in range(1, IDX_HEADS):
            score = score + weighted[h * SUB:(h + 1) * SUB, :]
        s_ref[0, j] = score


def _idx_decode(page_table, iq, iw, cache_ik, layer):
    nb, npg = page_table.shape
    pg = math.gcd(npg, 8)
    seq = lambda b, p, pt: (b, 0, 0)

    def page(j):
        return lambda b, p, pt: (layer, pt[b * npg + p * pg + j], 0, 0)

    grid_spec = pltpu.PrefetchScalarGridSpec(
        num_scalar_prefetch=1, grid=(nb, npg // pg),
        in_specs=[pl.BlockSpec((1, SUB, 1024), seq), pl.BlockSpec((1, SUB, BLK), seq)]
        + [pl.BlockSpec((None, None, IDX_DIM, BLK), page(j)) for j in range(pg)],
        out_specs=pl.BlockSpec((1, pg, SUB, BLK), lambda b, p, pt: (b, p, 0, 0)),
        scratch_shapes=[pltpu.VMEM((IDX_HEADS * SUB, IDX_DIM), F32), pltpu.VMEM((IDX_HEADS * SUB, BLK), F32)])
    return pl.pallas_call(
        functools.partial(_idx_decode_kernel, pg=pg), grid_spec=grid_spec,
        out_shape=jax.ShapeDtypeStruct((nb, npg, SUB, BLK), F32),
        compiler_params=_cparams(("parallel", "arbitrary")), name="idx_decode",
    )(page_table.reshape(-1), iq, iw, *([cache_ik] * pg))


def _dsa_decode_kernel(pt_ref, rb_ref, q_ref, iq_ref, iw_ref, ikn_ref, kn_ref, vn_ref, sc_ref, bkt_ref,
                       *rest, topk, npg, pg):
    kp_refs, vp_refs = rest[:pg], rest[pg:2 * pg]
    (o_ref, skey_ref, thr_ref, bias_ref, m_ref, l_ref, acc_ref, kpad_ref, vpad_ref, ikpad_ref) = rest[2 * pg:]
    p = pl.program_id(1)
    lane = lax.broadcasted_iota(I32, (1, BLK), 1)
    heads = _head_slices(DSA_HEADS)

    def attend(k_pages, v_pages, sels, biases):
        outs = _softmax_phased([q_ref[0, :, hs] for hs in heads],
                               [[kp[h].astype(BF16) for kp in k_pages] for h in range(DSA_HEADS)],
                               [[vp[h].astype(BF16) for vp in v_pages] for h in range(DSA_HEADS)],
                               biases, sels, [m_ref[h] for h in range(DSA_HEADS)],
                               [l_ref[h] for h in range(DSA_HEADS)], False)
        for h, (pv, alpha, m_new, l_new) in enumerate(outs):
            acc_ref[:, heads[h]] = alpha * acc_ref[:, heads[h]] + pv
            m_ref[h] = m_new
            l_ref[h] = l_new

    @pl.when(p == 0)
    def _():
        kpad_ref[...] = jnp.zeros_like(kpad_ref)
        vpad_ref[...] = jnp.zeros_like(vpad_ref)
        ikpad_ref[...] = jnp.zeros_like(ikpad_ref)
        kpad_ref[:, 0:SUB, :] = kn_ref[0]
        vpad_ref[:, 0:SUB, :] = vn_ref[0]
        ikpad_ref[0:SUB, :] = ikn_ref[0].astype(F32)
        iq = iq_ref[0]
        iw = iw_ref[0]
        ik = ikpad_ref[...].astype(BF16)
        score = jnp.zeros((SUB, BLK), F32)
        for h in range(IDX_HEADS):
            grp = iq[:, (h // 2) * BLK:(h // 2 + 1) * BLK]
            half = (lane < IDX_DIM) if h % 2 == 0 else (lane >= IDX_DIM)
            logits = _dot_nt(jnp.where(half, grp, jnp.zeros_like(grp)), ik) * (IDX_DIM ** -0.5)
            score = score + jnp.maximum(logits, 0.0) * (iw[:, h:h + 1] * (IDX_HEADS ** -0.5))
        row = lax.broadcasted_iota(I32, (SUB, BLK), 0)
        col = lax.broadcasted_iota(I32, (SUB, BLK), 1)
        causal = col <= row
        skey_ref[npg] = jnp.where(causal, _score_key(score), INT_MIN)
        skey_ref[0:npg] = _score_key(sc_ref[0])

        def count_ge(cand):
            hit = jnp.where(skey_ref[...] >= cand[None], 1.0, 0.0)
            return jnp.sum(jnp.sum(hit, axis=0), axis=-1, keepdims=True)

        thr = jnp.broadcast_to(_rank_threshold(count_ge, float(topk), (SUB, 1)), (SUB, BLK))
        thr_ref[...] = thr
        for h in range(DSA_HEADS):
            for t in range(2):
                bias_ref[2 * h + t] = _bias_tile(bkt_ref[t], rb_ref, h)
        m_ref[...] = jnp.full_like(m_ref, NEG_BIG)
        l_ref[...] = jnp.zeros_like(l_ref)
        acc_ref[...] = jnp.zeros_like(acc_ref)
        attend([kpad_ref], [vpad_ref], [(skey_ref[npg] >= thr) & causal],
               [[bias_ref[2 * h]] for h in range(DSA_HEADS)])

    thr = thr_ref[...]
    pages = [p * pg + j for j in range(pg)]
    attend(kp_refs, vp_refs, [skey_ref[pi] >= thr for pi in pages],
           [[jnp.where(pi == npg - 1, bias_ref[2 * h + 1], rb_ref[REL_BUCKETS - 1, h]) for pi in pages]
            for h in range(DSA_HEADS)])

    @pl.when(p == pl.num_programs(1) - 1)
    def _():
        for h, hs in enumerate(heads):
            o_ref[0, :, hs] = (acc_ref[:, hs] / l_ref[h]).astype(BF16)


def _dsa_decode(page_table, rel_bias, q, iq, iw, ik2n, kn, vn, scores, cache_k, cache_v, layer, n_new):
    nb, npg = page_table.shape
    topk = min(TOPK_MAX, (npg * BLK + n_new) // 4)
    bkt = jnp.asarray(_bucket_tiles(SUB, BLK))
    assert int(_t5_bucket_np(np.arange(BLK + 1, (npg + 1) * BLK + SUB)).min()) == REL_BUCKETS - 1
    pg = _pages_per_step(npg)
    seq = lambda b, p, pt: (b, 0, 0)
    seq4 = lambda b, p, pt: (b, 0, 0, 0)
    page_blk = (None, None, DSA_HEADS, BLK, HEAD_DIM)

    def page(j):
        return lambda b, p, pt: (layer, pt[b * npg + p * pg + j], 0, 0, 0)

    pages = [pl.BlockSpec(page_blk, page(j)) for j in range(pg)]
    grid_spec = pltpu.PrefetchScalarGridSpec(
        num_scalar_prefetch=1, grid=(nb, npg // pg),
        in_specs=[pl.BlockSpec(memory_space=pltpu.SMEM),
                  pl.BlockSpec((1, SUB, HD_DSA), seq), pl.BlockSpec((1, SUB, 1024), seq),
                  pl.BlockSpec((1, SUB, BLK), seq), pl.BlockSpec((1, SUB, BLK), seq),
                  pl.BlockSpec((1, DSA_HEADS, SUB, HEAD_DIM), seq4), pl.BlockSpec((1, DSA_HEADS, SUB, HEAD_DIM), seq4),
                  pl.BlockSpec((1, npg, SUB, BLK), seq4),
                  pl.BlockSpec((2, SUB, BLK), lambda b, p, pt: (0, 0, 0))] + pages + pages,
        out_specs=pl.BlockSpec((1, SUB, HD_DSA), seq),
        scratch_shapes=[pltpu.VMEM((npg + 1, SUB, BLK), I32), pltpu.VMEM((SUB, BLK), I32),
                        pltpu.VMEM((2 * DSA_HEADS, SUB, BLK), F32),
                        pltpu.VMEM((DSA_HEADS, SUB, BLK), F32), pltpu.VMEM((DSA_HEADS, SUB, BLK), F32),
                        pltpu.VMEM((SUB, HD_DSA), F32),
                        pltpu.VMEM((DSA_HEADS, BLK, HEAD_DIM), F32), pltpu.VMEM((DSA_HEADS, BLK, HEAD_DIM), F32),
                        pltpu.VMEM((BLK, BLK), F32)])
    return pl.pallas_call(
        functools.partial(_dsa_decode_kernel, topk=topk, npg=npg, pg=pg), grid_spec=grid_spec,
        out_shape=jax.ShapeDtypeStruct((nb, SUB, HD_DSA), BF16),
        compiler_params=_cparams(("parallel", "arbitrary")), name="dsa_decode",
    )(page_table.reshape(-1), rel_bias, q, iq, iw, ik2n, kn, vn, scores, bkt, *([cache_k] * pg), *([cache_v] * pg))


def _out_proj_kernel(x_ref, a_ref, b_ref, c_ref, w_ref, g_ref, y_ref, h_ref):
    y = x_ref[...] + _dot(a_ref[...], w_ref[0:HD_SB, :]) + _dot(b_ref[...], w_ref[HD_SB:HD_SB + 512, :]) \
        + _dot(c_ref[...], w_ref[HD_SB + 512:, :])
    y_ref[...] = y
    h_ref[...] = (_rms_rows(y) * g_ref[...]).astype(BF16)


def _out_proj(x, o_sb, o_ret, o_dsa, w_o, g_ffn, tm):
    t, d = x.shape
    row = lambda i: (i, 0)
    const = lambda i: (0, 0)
    return pl.pallas_call(
        _out_proj_kernel, grid=(t // tm,),
        in_specs=[pl.BlockSpec((tm, d), row), pl.BlockSpec((tm, HD_SB), row), pl.BlockSpec((tm, 512), row),
                  pl.BlockSpec((tm, HD_DSA), row), pl.BlockSpec((d, d), const, pipeline_mode=pl.Buffered(1)),
                  pl.BlockSpec((1, d), const)],
        out_specs=[pl.BlockSpec((tm, d), row), pl.BlockSpec((tm, d), row)],
        out_shape=[jax.ShapeDtypeStruct((t, d), F32), jax.ShapeDtypeStruct((t, d), BF16)],
        compiler_params=_cparams(("parallel",)), name="out_proj",
    )(x, o_sb, o_ret, o_dsa, w_o, g_ffn.reshape(1, -1))


def _ffn_kernel(x_ref, h_ref, wg_ref, wv_ref, wd_ref, cw_ref, cb_ref, pre_ref, y_ref, gate_ref,
                acc_ref, carry_ref, *, seq_rows, tiles_per_seq):
    i = pl.program_id(0)
    f = pl.program_id(1)
    tm = h_ref.shape[0]
    h = h_ref[...]
    gate = _dot(h, wg_ref[...])
    val = _dot(h, wv_ref[...])
    gate_ref[...] = gate[tm - gate_ref.shape[0]:tm, :]
    rid = lax.broadcasted_iota(I32, gate.shape, 0)
    g1 = pltpu.roll(gate, 1, 0)
    g2 = pltpu.roll(gate, 2, 0)
    if seq_rows is None:
        fs = pl.ds(pl.multiple_of(f * FFN_TILE, FFN_TILE), FFN_TILE)
        first = (i % tiles_per_seq) == 0
        prev = jnp.where(first, 0.0, carry_ref[:, fs])
        p1 = jnp.broadcast_to(prev[SUB - 1:SUB, :], gate.shape)
        p2 = jnp.broadcast_to(prev[SUB - 2:SUB - 1, :], gate.shape)
        g1 = jnp.where(rid == 0, p1, g1)
        g2 = jnp.where(rid == 0, p2, jnp.where(rid == 1, p1, g2))
        carry_ref[:, fs] = gate[tm - SUB:tm, :]
    else:
        pre = pre_ref[...]
        pos = rid % seq_rows
        g1 = jnp.where(pos == 0, pltpu.roll(pre, tm - 1, 0), g1)
        g2 = jnp.where(pos < 2, pre, g2)
    conv = cb_ref[...] + cw_ref[0:1, :] * g2 + cw_ref[1:2, :] * g1 + cw_ref[2:3, :] * gate
    act = (conv / (1.0 + jnp.exp(-conv)) * val).astype(BF16)
    part = _dot(act, wd_ref[...])

    @pl.when(f == 0)
    def _():
        acc_ref[...] = x_ref[...] + part

    @pl.when(f > 0)
    def _():
        acc_ref[...] += part

    @pl.when(f == pl.num_programs(1) - 1)
    def _():
        y_ref[...] = acc_ref[...]


def _ffn(x, h, wg, wv, wd, cw, cb, pre, tm, seq_rows, tiles_per_seq):
    t, d = x.shape
    gate_rows = SUB if seq_rows is None else tm
    nf = FFN_PAD // FFN_TILE
    row = lambda i, f: (i, 0)
    ftile = lambda i, f: (0, f)
    return pl.pallas_call(
        functools.partial(_ffn_kernel, seq_rows=seq_rows, tiles_per_seq=tiles_per_seq), grid=(t // tm, nf),
        in_specs=[pl.BlockSpec((tm, d), row), pl.BlockSpec((tm, d), row),
                  pl.BlockSpec((d, FFN_TILE), ftile), pl.BlockSpec((d, FFN_TILE), ftile),
                  pl.BlockSpec((FFN_TILE, d), lambda i, f: (f, 0)),
                  pl.BlockSpec((3, FFN_TILE), ftile), pl.BlockSpec((1, FFN_TILE), ftile),
                  pl.BlockSpec((pre.shape[0], FFN_TILE), ftile)],
        out_specs=[pl.BlockSpec((tm, d), row), pl.BlockSpec((gate_rows, FFN_TILE), lambda i, f: (i, f))],
        out_shape=[jax.ShapeDtypeStruct((t, d), F32), jax.ShapeDtypeStruct((t // tm * gate_rows, FFN_PAD), F32)],
        scratch_shapes=[pltpu.VMEM((tm, d), F32), pltpu.VMEM((SUB, FFN_PAD), F32)],
        compiler_params=_cparams(("arbitrary", "arbitrary")), name="ffn",
    )(x, h, wg, wv, wd, cw, cb, pre)


def kernel(x_prompt, x_sample, cache_sb_k, cache_sb_v, cache_dsa_k, cache_dsa_v, cache_idx_k, state_ret, state_conv,
           page_table, w_in, w_o, g_attn, g_ffn, g_q, g_k, rel_bias, w_up, conv_w, conv_b, w_down):
    nb, l, d = x_prompt.shape
    db, ds, _ = x_sample.shape
    depth = w_in.shape[0]
    n_pool, page = cache_sb_k.shape[1], cache_sb_k.shape[2]
    npg = page_table.shape[1]
    assert d == D_MODEL and page == BLK and l % 512 == 0 and ds <= SUB - 2
    past_len = npg * page
    tp, ts = nb * l, db * SUB

    heads_first = lambda c: jnp.transpose(c, (0, 1, 3, 2, 4))
    c_sb_k, c_sb_v = heads_first(cache_sb_k), heads_first(cache_sb_v)
    c_dsa_k, c_dsa_v = heads_first(cache_dsa_k), heads_first(cache_dsa_v)
    c_idx_k = jnp.transpose(cache_idx_k, (0, 1, 3, 2))
    tokens_first = lambda a: jnp.transpose(a, (0, 2, 1, 3))

    yp = x_prompt.reshape(tp, d)
    ys = jnp.pad(x_sample, ((0, 0), (0, SUB - ds), (0, 0))).reshape(ts, d)
    pos_p = np.arange(l)
    pos_s = past_len + np.arange(SUB)
    zero_state = jnp.zeros((nb, RET_HEADS, RET_QK, HEAD_DIM), F32)
    zero_pre = jnp.zeros((SUB, FFN_PAD), F32)

    outs_p = [[] for _ in range(7)]
    outs_s = [[] for _ in range(7)]
    for layer in range(depth):
        w_sb, w_ret, w_dsa = _prep_in_weights(w_in[layer])
        wo = w_o[layer].astype(BF16)
        wg = _pad_cols(w_up[layer][:, :FFN_DIM], FFN_PAD).astype(BF16)
        wv = _pad_cols(w_up[layer][:, FFN_DIM:], FFN_PAD).astype(BF16)
        wd = jnp.pad(w_down[layer], ((0, FFN_PAD - FFN_DIM), (0, 0))).astype(BF16)
        cw = _pad_cols(conv_w[layer], FFN_PAD)
        cb = _pad_cols(conv_b[layer].reshape(1, -1), FFN_PAD)

        sb, ret, dsa = _in_projections(yp, g_attn[layer], w_sb, w_ret, w_dsa, g_q[layer], g_k[layer], 512, l)
        o_sb = _sb_prompt(sb[0], sb[2], sb[4], nb, l)
        o_ret, s_ret = _retention(ret[0], ret[1], ret[2], ret[3], zero_state, pos_p, BLK, BLK, nb)
        o_dsa = _dsa_prompt(rel_bias, dsa[0], dsa[5], dsa[8], dsa[2], dsa[4], dsa[7], nb, l)
        yp, hp = _out_proj(yp, o_sb, o_ret, o_dsa, wo, g_ffn[layer], 512)
        yp, gate_p = _ffn(yp, hp, wg, wv, wd, cw, cb, zero_pre, 512, None, l // 512)
        for lst, val in zip(outs_p, (tokens_first(sb[1]), tokens_first(sb[3]), tokens_first(dsa[1]),
                                     tokens_first(dsa[3]), dsa[6].reshape(nb, l, IDX_DIM), s_ret,
                                     gate_p.reshape(nb, l // 512, SUB, FFN_PAD)[:, -1, SUB - 2:, :FFN_DIM])):
            lst.append(val)

        sb, ret, dsa = _in_projections(ys, g_attn[layer], w_sb, w_ret, w_dsa, g_q[layer], g_k[layer], ts, SUB)
        r3 = lambda a: a.reshape(db, SUB, a.shape[-1])
        o_sb = _sb_decode(page_table, r3(sb[0]), sb[1], sb[3], c_sb_k, c_sb_v, layer)
        o_ret, s_ret = _retention(ret[0], ret[1], ret[2], ret[3], state_ret[layer], pos_s, SUB, ds, db)
        scores = _idx_decode(page_table, r3(dsa[5]), r3(dsa[8]), c_idx_k, layer)
        o_dsa = _dsa_decode(page_table, rel_bias, r3(dsa[0]), r3(dsa[5]), r3(dsa[8]), r3(dsa[7]), dsa[1],
                            dsa[3], scores, c_dsa_k, c_dsa_v, layer, ds)
        ys, hs = _out_proj(ys, o_sb.reshape(ts, HD_SB), o_ret, o_dsa.reshape(ts, HD_DSA), wo, g_ffn[layer], ts)
        pre = jnp.pad(state_conv[layer], ((0, 0), (0, SUB - 2), (0, FFN_PAD - FFN_DIM))).reshape(ts, FFN_PAD)
        ys, gate_s = _ffn(ys, hs, wg, wv, wd, cw, cb, pre, ts, SUB, 1)
        cut = lambda a: tokens_first(a)[:, :ds]
        for lst, val in zip(outs_s, (cut(sb[1]), cut(sb[3]), cut(dsa[1]), cut(dsa[3]),
                                     dsa[6].reshape(db, SUB, IDX_DIM)[:, :ds], s_ret,
                                     gate_s.reshape(db, SUB, FFN_PAD)[:, ds - 2:ds, :FFN_DIM])):
            lst.append(val)

    stk = lambda lst: jnp.stack(lst)
    return ((yp.reshape(nb, l, d), ys.reshape(db, SUB, d)[:, :ds])
            + tuple(stk(o) for o in outs_p) + tuple(stk(o) for o in outs_s))
```

```python
import functools
import math

import numpy as np
import jax
import jax.numpy as jnp
from jax import lax
from jax.experimental import pallas as pl
from jax.experimental.pallas import tpu as pltpu

F32 = jnp.float32
BF16 = jnp.bfloat16
I32 = jnp.int32

D_MODEL = 2048
HEAD_DIM = 128
SB_HEADS = 6
RET_HEADS = 4
DSA_HEADS = 6
RET_QK = 64
RET_HALF = RET_QK // 2
IDX_HEADS = 16
IDX_DIM = 64
TOPK_MAX = 256
REL_BUCKETS = 32
REL_MAX_DIST = 128
FFN_DIM = 5504
FFN_PAD = 5632
FFN_TILE = 512
RMS_EPS = 1e-6
ROPE_BASE = 10000.0
BLK = 128
SUB = 8
W_SB = 3 * SB_HEADS * HEAD_DIM
W_RET = 2 * RET_HEADS * RET_QK + 2 * RET_HEADS * HEAD_DIM
W_DSA = 3 * DSA_HEADS * HEAD_DIM + IDX_HEADS * IDX_DIM + 2 * BLK
HD_SB = SB_HEADS * HEAD_DIM
HD_DSA = DSA_HEADS * HEAD_DIM
INT_MIN = np.int32(-2 ** 31)
NEG_BIG = -1e30
VMEM_LIMIT = 56 * 1024 * 1024

_NT = (((1,), (1,)), ((), ()))
_TN = (((0,), (0,)), ((), ()))


def _cparams(sem):
    return pltpu.CompilerParams(dimension_semantics=sem, vmem_limit_bytes=VMEM_LIMIT)


def _dot(a, b):
    return jnp.dot(a, b, preferred_element_type=F32)


def _dot_nt(a, b):
    return lax.dot_general(a, b, _NT, preferred_element_type=F32)


def _rms_rows(x):
    return x * lax.rsqrt(jnp.mean(x * x, axis=-1, keepdims=True) + RMS_EPS)


def _t5_bucket_np(dist):
    max_exact = REL_BUCKETS // 2
    dist = np.maximum(dist, 0)
    far = max_exact + (np.log(np.maximum(dist, 1).astype(np.float32) / np.float32(max_exact))
                       / np.float32(math.log(REL_MAX_DIST / max_exact))
                       * np.float32(REL_BUCKETS - max_exact)).astype(np.int32)
    return np.where(dist < max_exact, dist, np.minimum(far, REL_BUCKETS - 1)).astype(np.int32)


def _bucket_tiles(rows, base):
    r = np.arange(rows)[:, None]
    c = np.arange(BLK)[None, :]
    return np.stack([_t5_bucket_np(r - c), _t5_bucket_np(base + r - c)]).astype(np.int32)


def _rope_tables(pos):
    inv_freq = (1.0 / (np.float32(ROPE_BASE) ** np.linspace(0.0, 1.0, RET_HALF, dtype=np.float32))).astype(np.float32)
    ang = pos.astype(np.float32)[:, None] * inv_freq[None, :]
    cos = np.tile(np.cos(ang).astype(np.float32), (1, RET_HEADS))
    sin = np.tile(np.sin(ang).astype(np.float32), (1, RET_HEADS))
    return cos, sin


def _ret_tables(rows, n):
    lg = np.log(np.float32(1.0) - np.float32(2.0) ** (-5.0 - np.arange(RET_HEADS, dtype=np.float32))).astype(np.float32)
    i = np.arange(BLK, dtype=np.float32)
    diff = i[:, None] - i[None, :]
    valid = (np.arange(BLK) < n)
    dec = np.where(diff >= 0, np.exp(lg[:, None, None] * np.maximum(diff, 0.0)), 0.0)
    dec = (dec * valid[None, :, None] * valid[None, None, :])[:, :rows, :]
    qd = np.exp(lg[:, None] * (i + 1.0)[None, :]) * valid[None, :]
    kd = np.exp(lg[:, None] * np.maximum(n - 1.0 - i, 0.0)[None, :]) * valid[None, :]
    gn = np.exp(lg * np.float32(n))
    qd = np.broadcast_to(qd[:, :rows, None], (RET_HEADS, rows, BLK))
    kd = np.broadcast_to(kd[:, :, None], (RET_HEADS, BLK, BLK))
    return (dec.astype(np.float32), np.ascontiguousarray(qd, dtype=np.float32),
            np.ascontiguousarray(kd, dtype=np.float32), [float(g) for g in gn])


def _cumsum_matrix():
    r = np.arange(2 * BLK)[:, None] % BLK
    c = np.arange(2 * BLK)[None, :]
    return np.where(c < BLK, r > c, True).astype(np.float32)


def _prep_in_weights(w):
    perm = np.array([h * RET_QK + a * RET_HALF + i for a in range(2) for h in range(RET_HEADS) for i in range(RET_HALF)])
    w_sb = w[:, :2304]
    w_ret = jnp.concatenate([w[:, 2304:2560][:, perm], w[:, 2560:2816][:, perm], w[:, 2816:3840]], axis=1)
    ik = w[:, 7184:7248]
    w_dsa = jnp.concatenate([w[:, 3840:7168], ik, ik, w[:, 7168:7184],
                             jnp.zeros((w.shape[0], BLK - IDX_HEADS), w.dtype)], axis=1)
    return w_sb.astype(BF16), w_ret.astype(BF16), w_dsa.astype(BF16)


def _pad_cols(a, n):
    return jnp.pad(a, ((0, 0), (0, n - a.shape[1])))


def _norm_in(x_ref, g_ref):
    x = x_ref[...]
    return (_rms_rows(x) * g_ref[...]).astype(BF16)


def _store_heads(ref, a):
    tb, nh, lt, _ = ref.shape
    for h in range(nh):
        ref[:, h] = a[:, h * HEAD_DIM:(h + 1) * HEAD_DIM].reshape(tb, lt, HEAD_DIM)


def _proj_sb_kernel(x_ref, g_ref, w_ref, q_ref, k_ref, kb_ref, v_ref, vb_ref):
    h = _norm_in(x_ref, g_ref)
    q_ref[...] = _dot(h, w_ref[:, 0:HD_SB]).astype(BF16)
    k = _dot(h, w_ref[:, HD_SB:2 * HD_SB])
    _store_heads(k_ref, k)
    kb_ref[...] = k.astype(BF16)
    v = _dot(h, w_ref[:, 2 * HD_SB:3 * HD_SB])
    _store_heads(v_ref, v)
    vb_ref[...] = v.astype(BF16)


def _proj_ret_kernel(x_ref, g_ref, w_ref, q_ref, k_ref, v_ref, gate_ref):
    h = _norm_in(x_ref, g_ref)
    q_ref[...] = _dot(h, w_ref[:, 0:256])
    k_ref[...] = _dot(h, w_ref[:, 256:512])
    v_ref[...] = _dot(h, w_ref[:, 512:1024]).astype(BF16)
    gate_ref[...] = _dot(h, w_ref[:, 1024:1536])


def _proj_dsa_kernel(x_ref, g_ref, w_ref, gq_ref, gk_ref, q_ref, k_ref, kb_ref, v_ref, vb_ref,
                     iq_ref, ik_ref, ik2_ref, iw_ref):
    h = _norm_in(x_ref, g_ref)
    tb, _, lt, _ = k_ref.shape
    for hh in range(DSA_HEADS):
        lo, hi = hh * HEAD_DIM, (hh + 1) * HEAD_DIM
        q_ref[:, lo:hi] = (_rms_rows(_dot(h, w_ref[:, lo:hi])) * gq_ref[...]).astype(BF16)
        k = _rms_rows(_dot(h, w_ref[:, HD_DSA + lo:HD_DSA + hi])) * gk_ref[...]
        k_ref[:, hh] = k.reshape(tb, lt, HEAD_DIM)
        kb_ref[:, lo:hi] = k.astype(BF16)
    v = _dot(h, w_ref[:, 2 * HD_DSA:3 * HD_DSA])
    _store_heads(v_ref, v)
    vb_ref[...] = v.astype(BF16)
    c0 = 3 * HD_DSA
    iq_ref[...] = _dot(h, w_ref[:, c0:c0 + 1024]).astype(BF16)
    ik2 = _dot(h, w_ref[:, c0 + 1024:c0 + 1152])
    ik_ref[...] = ik2[:, 0:IDX_DIM]
    ik2_ref[...] = ik2.astype(BF16)
    iw_ref[...] = _dot(h, w_ref[:, c0 + 1152:c0 + 1280])


def _proj_call(kernel, x, g, w, extra, outs, tm, seq_rows, name):
    t, d = x.shape
    row = lambda i: (i, 0)
    const = lambda i: (0, 0)
    if tm >= seq_rows:
        tb, lt, head_idx = tm // seq_rows, seq_rows, (lambda i: (i, 0, 0, 0))
    else:
        tiles = seq_rows // tm
        tb, lt, head_idx = 1, tm, (lambda i: (i // tiles, 0, i % tiles, 0))
    in_specs = [pl.BlockSpec((tm, d), row), pl.BlockSpec((1, d), const),
                pl.BlockSpec(w.shape, const, pipeline_mode=pl.Buffered(1))]
    in_specs += [pl.BlockSpec(e.shape, const) for e in extra]
    out_specs, out_shape = [], []
    for kind, arg in outs:
        if kind == "heads":
            out_specs.append(pl.BlockSpec((tb, arg, lt, HEAD_DIM), head_idx))
            out_shape.append(jax.ShapeDtypeStruct((t // seq_rows, arg, seq_rows, HEAD_DIM), F32))
        else:
            out_specs.append(pl.BlockSpec((tm, kind), row))
            out_shape.append(jax.ShapeDtypeStruct((t, kind), arg))
    return pl.pallas_call(
        kernel, grid=(t // tm,), in_specs=in_specs, out_specs=out_specs, out_shape=out_shape,
        compiler_params=_cparams(("parallel",)), name=name,
    )(x, g, w, *extra)


def _in_projections(x, g_attn, w_sb, w_ret, w_dsa, g_q, g_k, tm, seq_rows):
    g = g_attn.reshape(1, -1)
    sb = _proj_call(_proj_sb_kernel, x, g, w_sb, [],
                    [(HD_SB, BF16), ("heads", SB_HEADS), (HD_SB, BF16), ("heads", SB_HEADS), (HD_SB, BF16)],
                    tm, seq_rows, "proj_sb")
    ret = _proj_call(_proj_ret_kernel, x, g, w_ret, [],
                     [(256, F32), (256, F32), (512, BF16), (512, F32)], tm, seq_rows, "proj_ret")
    dsa = _proj_call(_proj_dsa_kernel, x, g, w_dsa, [g_q.reshape(1, -1), g_k.reshape(1, -1)],
                     [(HD_DSA, BF16), ("heads", DSA_HEADS), (HD_DSA, BF16), ("heads", DSA_HEADS), (HD_DSA, BF16),
                      (1024, BF16), (IDX_DIM, F32), (BLK, BF16), (BLK, F32)], tm, seq_rows, "proj_dsa")
    return sb, ret, dsa


def _sb_phased(qs, ks, vs, u, runs, visibles):
    nh, nj = len(qs), len(visibles)
    z = [[_dot_nt(qs[h], ks[h][j]) * (HEAD_DIM ** -0.5) for j in range(nj)] for h in range(nh)]
    sp = [[None] * nj for _ in range(nh)]
    parts = []
    for h in range(nh):
        for j in range(nj):
            zz = z[h][j]
            s = jnp.maximum(zz, 0.0) + jnp.log(1.0 + jnp.exp(-jnp.abs(zz)))
            sp[h][j] = s
            parts.append(-s if visibles[j] is None else jnp.where(visibles[j], -s, 0.0))
    rows = qs[0].shape[0]
    groups = [parts] if rows < BLK else [[p] for p in parts]
    cs_flat = []
    for grp in groups:
        lk = grp[0] if len(grp) == 1 else jnp.concatenate(grp, axis=0)
        hi = lk.astype(BF16)
        lo = (lk - hi.astype(F32)).astype(BF16)
        c = _dot(jnp.concatenate([hi, lo], axis=1), u)
        cs_flat += [c[i * rows:(i + 1) * rows, :] for i in range(len(grp))]
    cs = [[cs_flat[h * nj + j] for j in range(nj)] for h in range(nh)]
    outs = []
    for h in range(nh):
        run, o = runs[h], None
        for j in range(nj):
            a = jnp.exp(z[h][j] - sp[h][j] + cs[h][j][:, :BLK] + run)
            if visibles[j] is not None:
                a = jnp.where(visibles[j], a, 0.0)
            pv = _dot(a.astype(BF16), vs[h][j])
            o = pv if o is None else o + pv
            run = run + cs[h][j][:, BLK:]
        outs.append((o, run))
    return outs


def _head_slices(n):
    return [slice(h * HEAD_DIM, (h + 1) * HEAD_DIM) for h in range(n)]


def _sb_prompt_kernel(q_ref, k_ref, v_ref, u_ref, o_ref, acc_ref, run_ref):
    qi = pl.program_id(1)
    u = u_ref[...]
    acc_ref[...] = jnp.zeros_like(acc_ref)
    run_ref[...] = jnp.zeros_like(run_ref)
    row = lax.broadcasted_iota(I32, (BLK, BLK), 0)
    col = lax.broadcasted_iota(I32, (BLK, BLK), 1)
    strict = col < row
    heads = _head_slices(SB_HEADS)

    def step(kbs, visibles):
        offs = [pl.multiple_of(kb * BLK, BLK) for kb in kbs]
        outs = _sb_phased([q_ref[:, hs] for hs in heads],
                          [[k_ref[pl.ds(off, BLK), hs] for off in offs] for hs in heads],
                          [[v_ref[pl.ds(off, BLK), hs] for off in offs] for hs in heads],
                          u, [run_ref[:, hs] for hs in heads], visibles)
        for hs, (o, run) in zip(heads, outs):
            acc_ref[:, hs] += o
            run_ref[:, hs] = run

    odd = qi % 2

    @pl.when(odd == 1)
    def _():
        step([qi, qi - 1], [strict, None])

    @pl.when(odd == 0)
    def _():
        step([qi], [strict])

    start = qi - 1 - odd

    def body(t, carry):
        step([start - 2 * t, start - 2 * t - 1], [None, None])
        return carry

    lax.fori_loop(0, (start + 1) // 2, body, 0)
    o_ref[...] = acc_ref[...].astype(BF16)


def _sb_prompt(q, k, v, nb, l):
    u = jnp.asarray(_cumsum_matrix(), BF16)
    nq = l // BLK
    return pl.pallas_call(
        _sb_prompt_kernel, grid=(nb, nq),
        in_specs=[pl.BlockSpec((BLK, HD_SB), lambda b, i: (b * nq + i, 0)),
                  pl.BlockSpec((l, HD_SB), lambda b, i: (b, 0)),
                  pl.BlockSpec((l, HD_SB), lambda b, i: (b, 0)),
                  pl.BlockSpec((2 * BLK, 2 * BLK), lambda b, i: (0, 0))],
        out_specs=pl.BlockSpec((BLK, HD_SB), lambda b, i: (b * nq + i, 0)),
        out_shape=jax.ShapeDtypeStruct((nb * l, HD_SB), BF16),
        scratch_shapes=[pltpu.VMEM((BLK, HD_SB), F32), pltpu.VMEM((BLK, HD_SB), F32)],
        compiler_params=_cparams(("parallel", "arbitrary")), name="sb_prompt",
    )(q, k, v, u)


def _pages_per_step(npg):
    return math.gcd(npg, 8)


def _sb_decode_kernel(pt_ref, q_ref, kn_ref, vn_ref, u_ref, *rest, pg):
    kp_refs, vp_refs = rest[:pg], rest[pg:2 * pg]
    o_ref, acc_ref, run_ref, kpad_ref, vpad_ref = rest[2 * pg:]
    p = pl.program_id(1)
    u = u_ref[...]
    heads = _head_slices(SB_HEADS)

    def step(k_pages, v_pages, visibles):
        outs = _sb_phased([q_ref[0, :, hs] for hs in heads],
                          [[kp[h].astype(BF16) for kp in k_pages] for h in range(SB_HEADS)],
                          [[vp[h].astype(BF16) for vp in v_pages] for h in range(SB_HEADS)],
                          u, [run_ref[:, hs] for hs in heads], visibles)
        for hs, (o, run) in zip(heads, outs):
            acc_ref[:, hs] += o
            run_ref[:, hs] = run

    @pl.when(p == 0)
    def _():
        acc_ref[...] = jnp.zeros_like(acc_ref)
        run_ref[...] = jnp.zeros_like(run_ref)
        kpad_ref[...] = jnp.zeros_like(kpad_ref)
        vpad_ref[...] = jnp.zeros_like(vpad_ref)
        kpad_ref[:, 0:SUB, :] = kn_ref[0]
        vpad_ref[:, 0:SUB, :] = vn_ref[0]
        row = lax.broadcasted_iota(I32, (SUB, BLK), 0)
        col = lax.broadcasted_iota(I32, (SUB, BLK), 1)
        step([kpad_ref], [vpad_ref], [col < row])

    step(kp_refs, vp_refs, [None] * pg)

    @pl.when(p == pl.num_programs(1) - 1)
    def _():
        o_ref[0] = acc_ref[...].astype(BF16)


def _sb_decode(page_table, q, kn, vn, cache_k, cache_v, layer):
    nb, npg = page_table.shape
    pg = _pages_per_step(npg)
    u = jnp.asarray(_cumsum_matrix(), BF16)
    seq = lambda b, p, pt: (b, 0, 0)
    seq4 = lambda b, p, pt: (b, 0, 0, 0)
    page_blk = (None, None, SB_HEADS, BLK, HEAD_DIM)

    def page(j):
        return lambda b, p, pt: (layer, pt[b * npg + (npg - 1 - (p * pg + j))], 0, 0, 0)

    pages = [pl.BlockSpec(page_blk, page(j)) for j in range(pg)]
    grid_spec = pltpu.PrefetchScalarGridSpec(
        num_scalar_prefetch=1, grid=(nb, npg // pg),
        in_specs=[pl.BlockSpec((1, SUB, HD_SB), seq), pl.BlockSpec((1, SB_HEADS, SUB, HEAD_DIM), seq4),
                  pl.BlockSpec((1, SB_HEADS, SUB, HEAD_DIM), seq4),
                  pl.BlockSpec((2 * BLK, 2 * BLK), lambda b, p, pt: (0, 0))] + pages + pages,
        out_specs=pl.BlockSpec((1, SUB, HD_SB), seq),
        scratch_shapes=[pltpu.VMEM((SUB, HD_SB), F32), pltpu.VMEM((SUB, HD_SB), F32),
                        pltpu.VMEM((SB_HEADS, BLK, HEAD_DIM), F32), pltpu.VMEM((SB_HEADS, BLK, HEAD_DIM), F32)])
    return pl.pallas_call(
        functools.partial(_sb_decode_kernel, pg=pg), grid_spec=grid_spec,
        out_shape=jax.ShapeDtypeStruct((nb, SUB, HD_SB), BF16),
        compiler_params=_cparams(("parallel", "arbitrary")), name="sb_decode",
    )(page_table.reshape(-1), q, kn, vn, u, *([cache_k] * pg), *([cache_v] * pg))


def _ret_kernel(q_ref, k_ref, v_ref, g_ref, cos_ref, sin_ref, dec_ref, qd_ref, kd_ref, s0_ref,
                o_ref, sout_ref, s_ref, kc_ref, vp_ref, *, gn):
    c = pl.program_id(1)
    rows = q_ref.shape[0]

    @pl.when(c == 0)
    def _():
        s_ref[...] = jnp.zeros_like(s_ref)
        kc_ref[...] = jnp.zeros_like(kc_ref)
        vp_ref[...] = jnp.zeros_like(vp_ref)
        for h in range(RET_HEADS):
            s_ref[h, h * RET_HALF:(h + 1) * RET_HALF, :] = s0_ref[0, h, 0:RET_HALF, :]
            s_ref[h, BLK + h * RET_HALF:BLK + (h + 1) * RET_HALF, :] = s0_ref[0, h, RET_HALF:RET_QK, :]

    cos = cos_ref[...]
    sin = sin_ref[...]
    q = q_ref[...]
    k = k_ref[...]
    q1 = q[:, :BLK] * cos - q[:, BLK:] * sin
    q2 = q[:, :BLK] * sin + q[:, BLK:] * cos
    k1 = (k[:, :BLK] * cos - k[:, BLK:] * sin) * (RET_QK ** -0.5)
    k2 = (k[:, :BLK] * sin + k[:, BLK:] * cos) * (RET_QK ** -0.5)
    kc_ref[0:rows, :] = jnp.concatenate([k1, k2], axis=1)
    vp_ref[0:rows, :] = v_ref[...].astype(F32)
    kc = kc_ref[...]
    kcat = kc.astype(BF16)
    lane = lax.broadcasted_iota(I32, (1, BLK), 1)
    for h in range(RET_HEADS):
        hs = slice(h * HEAD_DIM, (h + 1) * HEAD_DIM)
        m = jnp.where((lane >= h * RET_HALF) & (lane < (h + 1) * RET_HALF), 1.0, 0.0)
        qm = jnp.concatenate([q1 * m, q2 * m], axis=1).astype(BF16)
        km = (kc * jnp.concatenate([m, m], axis=1)).astype(BF16)
        v = vp_ref[:, hs]
        att = _dot_nt(qm, kcat) * dec_ref[h]
        s = s_ref[h]
        o = _dot(att.astype(BF16), v.astype(BF16)) + _dot(qm, s.astype(BF16)) * qd_ref[h]
        vd = (v * kd_ref[h]).astype(BF16)
        s_ref[h] = gn[h] * s + lax.dot_general(km, vd, _TN, preferred_element_type=F32)
        gate = g_ref[:, hs]
        o_ref[:, hs] = (_rms_rows(o) * (gate / (1.0 + jnp.exp(-gate)))).astype(BF16)

    @pl.when(c == pl.num_programs(1) - 1)
    def _():
        for h in range(RET_HEADS):
            sout_ref[0, h, 0:RET_HALF, :] = s_ref[h, h * RET_HALF:(h + 1) * RET_HALF, :]
            sout_ref[0, h, RET_HALF:RET_QK, :] = s_ref[h, BLK + h * RET_HALF:BLK + (h + 1) * RET_HALF, :]


def _retention(q, k, v, g, s0, pos, rows, n, nb):
    nc = q.shape[0] // (nb * rows)
    cos, sin = _rope_tables(pos)
    dec, qd, kd, gn = _ret_tables(rows, n)
    tok = lambda b, c: (b * nc + c, 0)
    full3 = lambda b, c: (0, 0, 0)
    return pl.pallas_call(
        functools.partial(_ret_kernel, gn=gn), grid=(nb, nc),
        in_specs=[pl.BlockSpec((rows, 256), tok), pl.BlockSpec((rows, 256), tok),
                  pl.BlockSpec((rows, 512), tok), pl.BlockSpec((rows, 512), tok),
                  pl.BlockSpec((rows, BLK), lambda b, c: (c, 0)), pl.BlockSpec((rows, BLK), lambda b, c: (c, 0)),
                  pl.BlockSpec((RET_HEADS, rows, BLK), full3), pl.BlockSpec((RET_HEADS, rows, BLK), full3),
                  pl.BlockSpec((RET_HEADS, BLK, BLK), full3),
                  pl.BlockSpec((1, RET_HEADS, RET_QK, HEAD_DIM), lambda b, c: (b, 0, 0, 0))],
        out_specs=[pl.BlockSpec((rows, 512), tok),
                   pl.BlockSpec((1, RET_HEADS, RET_QK, HEAD_DIM), lambda b, c: (b, 0, 0, 0))],
        out_shape=[jax.ShapeDtypeStruct((q.shape[0], 512), BF16),
                   jax.ShapeDtypeStruct((nb, RET_HEADS, RET_QK, HEAD_DIM), F32)],
        scratch_shapes=[pltpu.VMEM((RET_HEADS, 2 * BLK, HEAD_DIM), F32), pltpu.VMEM((BLK, 2 * BLK), F32),
                        pltpu.VMEM((BLK, RET_HEADS * HEAD_DIM), F32)],
        compiler_params=_cparams(("parallel", "arbitrary")), name="retention",
    )(q, k, v, g, jnp.asarray(cos), jnp.asarray(sin), jnp.asarray(dec), jnp.asarray(qd), jnp.asarray(kd), s0)


def _score_key(score):
    score = jnp.where(score == 0.0, 0.0, score)
    bits = lax.bitcast_convert_type(score, I32)
    return jnp.where(bits < 0, bits ^ np.int32(0x7FFFFFFF), bits)


def _rank_threshold(count_ge, topk, shape):
    ans = jnp.zeros(shape, I32)
    for bit in range(31, -1, -1):
        cand = ans | np.int32(-2 ** 31 if bit == 31 else 2 ** bit)
        ans = jnp.where(count_ge(cand ^ INT_MIN) >= topk, cand, ans)
    return ans ^ INT_MIN


def _bias_tile(bkt, rb_ref, h):
    t = jnp.zeros(bkt.shape, F32)
    for b in range(REL_BUCKETS):
        t = jnp.where(bkt == b, rb_ref[b, h], t)
    return t


def _softmax_phased(qs, ks, vs, biases, sels, m_olds, l_olds, mxu_rowsum):
    nh, nj = len(qs), len(sels)
    s = [[jnp.where(sels[j], _dot_nt(qs[h], ks[h][j]) * (HEAD_DIM ** -0.5) + biases[h][j], NEG_BIG)
          for j in range(nj)] for h in range(nh)]
    stats, probs = [], []
    for h in range(nh):
        mx = s[h][0]
        for j in range(1, nj):
            mx = jnp.maximum(mx, s[h][j])
        m_new = jnp.maximum(m_olds[h], jnp.max(mx, axis=-1, keepdims=True))
        stats.append((jnp.exp(m_olds[h] - m_new), m_new))
        probs.append([jnp.where(sels[j], jnp.exp(s[h][j] - m_new), 0.0) for j in range(nj)])
    outs = []
    for h in range(nh):
        alpha, m_new = stats[h]
        if mxu_rowsum:
            ones = jnp.ones((BLK, BLK), BF16)
            p_cat = jnp.concatenate([p.astype(BF16) for p in probs[h]], axis=1)
            v_ext = jnp.concatenate([jnp.concatenate([vs[h][j], ones], axis=1) for j in range(nj)], axis=0)
            r = _dot(p_cat, v_ext)
            pv, rs = r[:, :BLK], r[:, BLK:]
        else:
            pv, tot = None, None
            for j in range(nj):
                d = _dot(probs[h][j].astype(BF16), vs[h][j])
                pv = d if pv is None else pv + d
                tot = probs[h][j] if tot is None else tot + probs[h][j]
            rs = jnp.sum(tot, axis=-1, keepdims=True)
        outs.append((pv, alpha, m_new, alpha * l_olds[h] + rs))
    return outs


def _dsa_prompt_kernel(rb_ref, q_ref, iq_ref, iw_ref, k_ref, v_ref, ik_ref, bkt_ref, o_ref,
                       skey_ref, skt_ref, wib_ref, qm_ref, bias_ref, m_ref, l_ref, acc_ref, *, topk):
    qi = pl.program_id(1)
    ca = qi // 2
    row = lax.broadcasted_iota(I32, (BLK, BLK), 0)
    col = lax.broadcasted_iota(I32, (BLK, BLK), 1)
    lane = lax.broadcasted_iota(I32, (1, BLK), 1)
    heads = _head_slices(DSA_HEADS)

    def causal(kb):
        return (col + kb * BLK) <= (row + qi * BLK)

    @pl.when((pl.program_id(0) == 0) & (qi == 0))
    def _():
        for h in range(DSA_HEADS):
            for t in range(2):
                bias_ref[2 * h + t] = _bias_tile(bkt_ref[t], rb_ref, h)

    iw = iw_ref[...]
    for g in range(IDX_HEADS // 2):
        qm_ref[g * BLK:(g + 1) * BLK, :] = iq_ref[:, g * BLK:(g + 1) * BLK]
    for h in range(IDX_HEADS):
        wib_ref[h] = jnp.broadcast_to(iw[:, h:h + 1] * (IDX_HEADS ** -0.5 * IDX_DIM ** -0.5), (BLK, BLK))
    first_half = lane < IDX_DIM

    def score_pair(c, diag):
        off = pl.multiple_of(c * 2 * BLK, 2 * BLK)
        ik = ik_ref[pl.ds(off, 2 * BLK), :]
        zero = jnp.zeros((BLK, BLK), BF16)
        for t in range(2):
            ikt = ik[t * BLK:(t + 1) * BLK, :]
            both = jnp.concatenate([jnp.where(first_half, ikt, zero), jnp.where(first_half, zero, ikt)], axis=0)
            logits = _dot_nt(qm_ref[...], both)
            score = jnp.zeros((BLK, BLK), F32)
            for g in range(IDX_HEADS // 2):
                lg = logits[g * BLK:(g + 1) * BLK, :]
                score = (score + jnp.maximum(lg[:, :BLK], 0.0) * wib_ref[2 * g]
                         + jnp.maximum(lg[:, BLK:], 0.0) * wib_ref[2 * g + 1])
            kt = _score_key(score)
            if diag:
                kt = jnp.where(causal(2 * c + t), kt, INT_MIN)
            skey_ref[2 * c + t] = kt
            skt_ref[2 * c + t] = kt.T

    score_pair(ca, True)

    def score_body(c, carry):
        score_pair(c, False)
        return carry

    lax.fori_loop(0, ca, score_body, 0)

    def count_ge(cand):
        def body(c, acc):
            for t in range(2):
                kt = skt_ref[2 * c + t].reshape(BLK // SUB, SUB, BLK)
                acc = acc + jnp.sum(jnp.where(kt >= cand[None], 1.0, 0.0), axis=0)
            return acc
        acc = lax.fori_loop(0, ca + 1, body, jnp.zeros((SUB, BLK), F32))
        return jnp.broadcast_to(jnp.sum(acc, axis=0, keepdims=True), (SUB, BLK))

    thr_row = _rank_threshold(count_ge, float(topk), (SUB, BLK))
    thr = jnp.broadcast_to(thr_row[0:1, :], (BLK, BLK)).T

    m_ref[...] = jnp.full_like(m_ref, NEG_BIG)
    l_ref[...] = jnp.zeros_like(l_ref)
    acc_ref[...] = jnp.zeros_like(acc_ref)

    def attend(c, near):
        kbs = [2 * c, 2 * c + 1]
        offs = [pl.multiple_of(kb * BLK, BLK) for kb in kbs]
        sels = [skey_ref[kb] >= thr for kb in kbs]
        far = [rb_ref[REL_BUCKETS - 1, h] for h in range(DSA_HEADS)]
        if near:
            sels = [sel & causal(kb) for kb, sel in zip(kbs, sels)]
            biases = [[jnp.where(qi - kb == 0, bias_ref[2 * h], jnp.where(qi - kb == 1, bias_ref[2 * h + 1], far[h]))
                       for kb in kbs] for h in range(DSA_HEADS)]
        else:
            biases = [[far[h]] * 2 for h in range(DSA_HEADS)]
        outs = _softmax_phased([q_ref[:, hs] for hs in heads],
                               [[k_ref[pl.ds(off, BLK), hs] for off in offs] for hs in heads],
                               [[v_ref[pl.ds(off, BLK), hs] for off in offs] for hs in heads],
                               biases, sels, [m_ref[h] for h in range(DSA_HEADS)],
                               [l_ref[h] for h in range(DSA_HEADS)], True)
        for h, (pv, alpha, m_new, l_new) in enumerate(outs):
            acc_ref[:, heads[h]] = alpha * acc_ref[:, heads[h]] + pv
            m_ref[h] = m_new
            l_ref[h] = l_new

    attend(ca, True)

    @pl.when(ca >= 1)
    def _():
        attend(ca - 1, True)

    def far_body(c, carry):
        attend(c, False)
        return carry

    lax.fori_loop(0, ca - 1, far_body, 0)

    for h, hs in enumerate(heads):
        o_ref[:, hs] = (acc_ref[:, hs] / l_ref[h]).astype(BF16)


def _dsa_prompt(rel_bias, q, iq, iw, k, v, ik2, nb, l):
    nq = l // BLK
    topk = min(TOPK_MAX, l // 4)
    bkt = jnp.asarray(_bucket_tiles(BLK, BLK))
    assert int(_t5_bucket_np(np.arange(BLK + 1, 4 * BLK)).min()) == REL_BUCKETS - 1
    tok = lambda b, i: (b * nq + i, 0)
    seq = lambda b, i: (b, 0)
    return pl.pallas_call(
        functools.partial(_dsa_prompt_kernel, topk=topk), grid=(nb, nq),
        in_specs=[pl.BlockSpec(memory_space=pltpu.SMEM),
                  pl.BlockSpec((BLK, HD_DSA), tok), pl.BlockSpec((BLK, 1024), tok), pl.BlockSpec((BLK, BLK), tok),
                  pl.BlockSpec((l, HD_DSA), seq), pl.BlockSpec((l, HD_DSA), seq), pl.BlockSpec((l, BLK), seq),
                  pl.BlockSpec((2, BLK, BLK), lambda b, i: (0, 0, 0))],
        out_specs=pl.BlockSpec((BLK, HD_DSA), tok),
        out_shape=jax.ShapeDtypeStruct((nb * l, HD_DSA), BF16),
        scratch_shapes=[pltpu.VMEM((nq, BLK, BLK), I32), pltpu.VMEM((nq, BLK, BLK), I32),
                        pltpu.VMEM((IDX_HEADS, BLK, BLK), F32),
                        pltpu.VMEM((IDX_HEADS // 2 * BLK, BLK), BF16), pltpu.VMEM((2 * DSA_HEADS, BLK, BLK), F32),
                        pltpu.VMEM((DSA_HEADS, BLK, BLK), F32), pltpu.VMEM((DSA_HEADS, BLK, BLK), F32),
                        pltpu.VMEM((BLK, HD_DSA), F32)],
        compiler_params=_cparams(("arbitrary", "arbitrary")), name="dsa_prompt",
    )(rel_bias, q, iq, iw, k, v, ik2, bkt)


def _idx_decode_kernel(pt_ref, iq_ref, iw_ref, *rest, pg):
    kp_refs = rest[:pg]
    s_ref, qa_ref, wib_ref = rest[pg:]
    p = pl.program_id(1)

    @pl.when(p == 0)
    def _():
        iq = iq_ref[0].astype(F32)
        iw = iw_ref[0]
        for h in range(IDX_HEADS):
            grp = iq[:, (h // 2) * BLK:(h // 2 + 1) * BLK]
            if h % 2 == 1:
                grp = pltpu.roll(grp, IDX_DIM, 1)
            qa_ref[h * SUB:(h + 1) * SUB, :] = grp[:, 0:IDX_DIM]
            wib_ref[h * SUB:(h + 1) * SUB, :] = jnp.broadcast_to(
                iw[:, h:h + 1] * (IDX_HEADS ** -0.5 * IDX_DIM ** -0.5), (SUB, BLK))

    qa = qa_ref[...].astype(BF16)
    logits = [_dot(qa, kp[...].astype(BF16)) for kp in kp_refs]
    for j in range(pg):
        weighted = jnp.maximum(logits[j], 0.0) * wib_ref[...]
        score = weighted[0:SUB, :]
        for h in range(1, IDX_HEADS):
            score = score + weighted[h * SUB:(h + 1) * SUB, :]
        s_ref[0, j] = score


def _idx_decode(page_table, iq, iw, cache_ik, layer):
    nb, npg = page_table.shape
    pg = math.gcd(npg, 8)
    seq = lambda b, p, pt: (b, 0, 0)

    def page(j):
        return lambda b, p, pt: (layer, pt[b * npg + p * pg + j], 0, 0)

    grid_spec = pltpu.PrefetchScalarGridSpec(
        num_scalar_prefetch=1, grid=(nb, npg // pg),
        in_specs=[pl.BlockSpec((1, SUB, 1024), seq), pl.BlockSpec((1, SUB, BLK), seq)]
        + [pl.BlockSpec((None, None, IDX_DIM, BLK), page(j)) for j in range(pg)],
        out_specs=pl.BlockSpec((1, pg, SUB, BLK), lambda b, p, pt: (b, p, 0, 0)),
        scratch_shapes=[pltpu.VMEM((IDX_HEADS * SUB, IDX_DIM), F32), pltpu.VMEM((IDX_HEADS * SUB, BLK), F32)])
    return pl.pallas_call(
        functools.partial(_idx_decode_kernel, pg=pg), grid_spec=grid_spec,
        out_shape=jax.ShapeDtypeStruct((nb, npg, SUB, BLK), F32),
        compiler_params=_cparams(("parallel", "arbitrary")), name="idx_decode",
    )(page_table.reshape(-1), iq, iw, *([cache_ik] * pg))


def _dsa_decode_kernel(pt_ref, rb_ref, q_ref, iq_ref, iw_ref, ikn_ref, kn_ref, vn_ref, sc_ref, bkt_ref,
                       *rest, topk, npg, pg):
    kp_refs, vp_refs = rest[:pg], rest[pg:2 * pg]
    (o_ref, skey_ref, thr_ref, bias_ref, m_ref, l_ref, acc_ref, kpad_ref, vpad_ref, ikpad_ref) = rest[2 * pg:]
    p = pl.program_id(1)
    lane = lax.broadcasted_iota(I32, (1, BLK), 1)
    heads = _head_slices(DSA_HEADS)

    def attend(k_pages, v_pages, sels, biases):
        outs = _softmax_phased([q_ref[0, :, hs] for hs in heads],
                               [[kp[h].astype(BF16) for kp in k_pages] for h in range(DSA_HEADS)],
                               [[vp[h].astype(BF16) for vp in v_pages] for h in range(DSA_HEADS)],
                               biases, sels, [m_ref[h] for h in range(DSA_HEADS)],
                               [l_ref[h] for h in range(DSA_HEADS)], False)
        for h, (pv, alpha, m_new, l_new) in enumerate(outs):
            acc_ref[:, heads[h]] = alpha * acc_ref[:, heads[h]] + pv
            m_ref[h] = m_new
            l_ref[h] = l_new

    @pl.when(p == 0)
    def _():
        kpad_ref[...] = jnp.zeros_like(kpad_ref)
        vpad_ref[...] = jnp.zeros_like(vpad_ref)
        ikpad_ref[...] = jnp.zeros_like(ikpad_ref)
        kpad_ref[:, 0:SUB, :] = kn_ref[0]
        vpad_ref[:, 0:SUB, :] = vn_ref[0]
        ikpad_ref[0:SUB, :] = ikn_ref[0].astype(F32)
        iq = iq_ref[0]
        iw = iw_ref[0]
        ik = ikpad_ref[...].astype(BF16)
        score = jnp.zeros((SUB, BLK), F32)
        for h in range(IDX_HEADS):
            grp = iq[:, (h // 2) * BLK:(h // 2 + 1) * BLK]
            half = (lane < IDX_DIM) if h % 2 == 0 else (lane >= IDX_DIM)
            logits = _dot_nt(jnp.where(half, grp, jnp.zeros_like(grp)), ik) * (IDX_DIM ** -0.5)
            score = score + jnp.maximum(logits, 0.0) * (iw[:, h:h + 1] * (IDX_HEADS ** -0.5))
        row = lax.broadcasted_iota(I32, (SUB, BLK), 0)
        col = lax.broadcasted_iota(I32, (SUB, BLK), 1)
        causal = col <= row
        skey_ref[npg] = jnp.where(causal, _score_key(score), INT_MIN)
        skey_ref[0:npg] = _score_key(sc_ref[0])

        def count_ge(cand):
            hit = jnp.where(skey_ref[...] >= cand[None], 1.0, 0.0)
            return jnp.sum(jnp.sum(hit, axis=0), axis=-1, keepdims=True)

        thr = jnp.broadcast_to(_rank_threshold(count_ge, float(topk), (SUB, 1)), (SUB, BLK))
        thr_ref[...] = thr
        for h in range(DSA_HEADS):
            for t in range(2):
                bias_ref[2 * h + t] = _bias_tile(bkt_ref[t], rb_ref, h)
        m_ref[...] = jnp.full_like(m_ref, NEG_BIG)
        l_ref[...] = jnp.zeros_like(l_ref)
        acc_ref[...] = jnp.zeros_like(acc_ref)
        attend([kpad_ref], [vpad_ref], [(skey_ref[npg] >= thr) & causal],
               [[bias_ref[2 * h]] for h in range(DSA_HEADS)])

    thr = thr_ref[...]
    pages = [p * pg + j for j in range(pg)]
    attend(kp_refs, vp_refs, [skey_ref[pi] >= thr for pi in pages],
           [[jnp.where(pi == npg - 1, bias_ref[2 * h + 1], rb_ref[REL_BUCKETS - 1, h]) for pi in pages]
            for h in range(DSA_HEADS)])

    @pl.when(p == pl.num_programs(1) - 1)
    def _():
        for h, hs in enumerate(heads):
            o_ref[0, :, hs] = (acc_ref[:, hs] / l_ref[h]).astype(BF16)


def _dsa_decode(page_table, rel_bias, q, iq, iw, ik2n, kn, vn, scores, cache_k, cache_v, layer, n_new):
    nb, npg = page_table.shape
    topk = min(TOPK_MAX, (npg * BLK + n_new) // 4)
    bkt = jnp.asarray(_bucket_tiles(SUB, BLK))
    assert int(_t5_bucket_np(np.arange(BLK + 1, (npg + 1) * BLK + SUB)).min()) == REL_BUCKETS - 1
    pg = _pages_per_step(npg)
    seq = lambda b, p, pt: (b, 0, 0)
    seq4 = lambda b, p, pt: (b, 0, 0, 0)
    page_blk = (None, None, DSA_HEADS, BLK, HEAD_DIM)

    def page(j):
        return lambda b, p, pt: (layer, pt[b * npg + p * pg + j], 0, 0, 0)

    pages = [pl.BlockSpec(page_blk, page(j)) for j in range(pg)]
    grid_spec = pltpu.PrefetchScalarGridSpec(
        num_scalar_prefetch=1, grid=(nb, npg // pg),
        in_specs=[pl.BlockSpec(memory_space=pltpu.SMEM),
                  pl.BlockSpec((1, SUB, HD_DSA), seq), pl.BlockSpec((1, SUB, 1024), seq),
                  pl.BlockSpec((1, SUB, BLK), seq), pl.BlockSpec((1, SUB, BLK), seq),
                  pl.BlockSpec((1, DSA_HEADS, SUB, HEAD_DIM), seq4), pl.BlockSpec((1, DSA_HEADS, SUB, HEAD_DIM), seq4),
                  pl.BlockSpec((1, npg, SUB, BLK), seq4),
                  pl.BlockSpec((2, SUB, BLK), lambda b, p, pt: (0, 0, 0))] + pages + pages,
        out_specs=pl.BlockSpec((1, SUB, HD_DSA), seq),
        scratch_shapes=[pltpu.VMEM((npg + 1, SUB, BLK), I32), pltpu.VMEM((SUB, BLK), I32),
                        pltpu.VMEM((2 * DSA_HEADS, SUB, BLK), F32),
                        pltpu.VMEM((DSA_HEADS, SUB, BLK), F32), pltpu.VMEM((DSA_HEADS, SUB, BLK), F32),
                        pltpu.VMEM((SUB, HD_DSA), F32),
                        pltpu.VMEM((DSA_HEADS, BLK, HEAD_DIM), F32), pltpu.VMEM((DSA_HEADS, BLK, HEAD_DIM), F32),
                        pltpu.VMEM((BLK, BLK), F32)])
    return pl.pallas_call(
        functools.partial(_dsa_decode_kernel, topk=topk, npg=npg, pg=pg), grid_spec=grid_spec,
        out_shape=jax.ShapeDtypeStruct((nb, SUB, HD_DSA), BF16),
        compiler_params=_cparams(("parallel", "arbitrary")), name="dsa_decode",
    )(page_table.reshape(-1), rel_bias, q, iq, iw, ik2n, kn, vn, scores, bkt, *([cache_k] * pg), *([cache_v] * pg))


def _out_proj_kernel(x_ref, a_ref, b_ref, c_ref, w_ref, g_ref, y_ref, h_ref):
    y = x_ref[...] + _dot(a_ref[...], w_ref[0:HD_SB, :]) + _dot(b_ref[...], w_ref[HD_SB:HD_SB + 512, :]) \
        + _dot(c_ref[...], w_ref[HD_SB + 512:, :])
    y_ref[...] = y
    h_ref[...] = (_rms_rows(y) * g_ref[...]).astype(BF16)


def _out_proj(x, o_sb, o_ret, o_dsa, w_o, g_ffn, tm):
    t, d = x.shape
    row = lambda i: (i, 0)
    const = lambda i: (0, 0)
    return pl.pallas_call(
        _out_proj_kernel, grid=(t // tm,),
        in_specs=[pl.BlockSpec((tm, d), row), pl.BlockSpec((tm, HD_SB), row), pl.BlockSpec((tm, 512), row),
                  pl.BlockSpec((tm, HD_DSA), row), pl.BlockSpec((d, d), const, pipeline_mode=pl.Buffered(1)),
                  pl.BlockSpec((1, d), const)],
        out_specs=[pl.BlockSpec((tm, d), row), pl.BlockSpec((tm, d), row)],
        out_shape=[jax.ShapeDtypeStruct((t, d), F32), jax.ShapeDtypeStruct((t, d), BF16)],
        compiler_params=_cparams(("parallel",)), name="out_proj",
    )(x, o_sb, o_ret, o_dsa, w_o, g_ffn.reshape(1, -1))


def _ffn_kernel(x_ref, h_ref, wg_ref, wv_ref, wd_ref, cw_ref, cb_ref, pre_ref, y_ref, gate_ref,
                acc_ref, carry_ref, *, seq_rows, tiles_per_seq):
    i = pl.program_id(0)
    f = pl.program_id(1)
    tm = h_ref.shape[0]
    h = h_ref[...]
    gate = _dot(h, wg_ref[...])
    val = _dot(h, wv_ref[...])
    gate_ref[...] = gate[tm - gate_ref.shape[0]:tm, :]
    rid = lax.broadcasted_iota(I32, gate.shape, 0)
    g1 = pltpu.roll(gate, 1, 0)
    g2 = pltpu.roll(gate, 2, 0)
    if seq_rows is None:
        fs = pl.ds(pl.multiple_of(f * FFN_TILE, FFN_TILE), FFN_TILE)
        first = (i % tiles_per_seq) == 0
        prev = jnp.where(first, 0.0, carry_ref[:, fs])
        p1 = jnp.broadcast_to(prev[SUB - 1:SUB, :], gate.shape)
        p2 = jnp.broadcast_to(prev[SUB - 2:SUB - 1, :], gate.shape)
        g1 = jnp.where(rid == 0, p1, g1)
        g2 = jnp.where(rid == 0, p2, jnp.where(rid == 1, p1, g2))
        carry_ref[:, fs] = gate[tm - SUB:tm, :]
    else:
        pre = pre_ref[...]
        pos = rid % seq_rows
        g1 = jnp.where(pos == 0, pltpu.roll(pre, tm - 1, 0), g1)
        g2 = jnp.where(pos < 2, pre, g2)
    conv = cb_ref[...] + cw_ref[0:1, :] * g2 + cw_ref[1:2, :] * g1 + cw_ref[2:3, :] * gate
    act = (conv / (1.0 + jnp.exp(-conv)) * val).astype(BF16)
    part = _dot(act, wd_ref[...])

    @pl.when(f == 0)
    def _():
        acc_ref[...] = x_ref[...] + part

    @pl.when(f > 0)
    def _():
        acc_ref[...] += part

    @pl.when(f == pl.num_programs(1) - 1)
    def _():
        y_ref[...] = acc_ref[...]


def _ffn(x, h, wg, wv, wd, cw, cb, pre, tm, seq_rows, tiles_per_seq):
    t, d = x.shape
    gate_rows = SUB if seq_rows is None else tm
    nf = FFN_PAD // FFN_TILE
    row = lambda i, f: (i, 0)
    ftile = lambda i, f: (0, f)
    return pl.pallas_call(
        functools.partial(_ffn_kernel, seq_rows=seq_rows, tiles_per_seq=tiles_per_seq), grid=(t // tm, nf),
        in_specs=[pl.BlockSpec((tm, d), row), pl.BlockSpec((tm, d), row),
                  pl.BlockSpec((d, FFN_TILE), ftile), pl.BlockSpec((d, FFN_TILE), ftile),
                  pl.BlockSpec((FFN_TILE, d), lambda i, f: (f, 0)),
                  pl.BlockSpec((3, FFN_TILE), ftile), pl.BlockSpec((1, FFN_TILE), ftile),
                  pl.BlockSpec((pre.shape[0], FFN_TILE), ftile)],
        out_specs=[pl.BlockSpec((tm, d), row), pl.BlockSpec((gate_rows, FFN_TILE), lambda i, f: (i, f))],
        out_shape=[jax.ShapeDtypeStruct((t, d), F32), jax.ShapeDtypeStruct((t // tm * gate_rows, FFN_PAD), F32)],
        scratch_shapes=[pltpu.VMEM((tm, d), F32), pltpu.VMEM((SUB, FFN_PAD), F32)],
        compiler_params=_cparams(("arbitrary", "arbitrary")), name="ffn",
    )(x, h, wg, wv, wd, cw, cb, pre)


def kernel(x_prompt, x_sample, cache_sb_k, cache_sb_v, cache_dsa_k, cache_dsa_v, cache_idx_k, state_ret, state_conv,
           page_table, w_in, w_o, g_attn, g_ffn, g_q, g_k, rel_bias, w_up, conv_w, conv_b, w_down):
    nb, l, d = x_prompt.shape
    db, ds, _ = x_sample.shape
    depth = w_in.shape[0]
    n_pool, page = cache_sb_k.shape[1], cache_sb_k.shape[2]
    npg = page_table.shape[1]
    assert d == D_MODEL and page == BLK and l % 512 == 0 and ds <= SUB - 2
    past_len = npg * page
    tp, ts = nb * l, db * SUB

    heads_first = lambda c: jnp.transpose(c, (0, 1, 3, 2, 4))
    c_sb_k, c_sb_v = heads_first(cache_sb_k), heads_first(cache_sb_v)
    c_dsa_k, c_dsa_v = heads_first(cache_dsa_k), heads_first(cache_dsa_v)
    c_idx_k = jnp.transpose(cache_idx_k, (0, 1, 3, 2))
    tokens_first = lambda a: jnp.transpose(a, (0, 2, 1, 3))

    yp = x_prompt.reshape(tp, d)
    ys = jnp.pad(x_sample, ((0, 0), (0, SUB - ds), (0, 0))).reshape(ts, d)
    pos_p = np.arange(l)
    pos_s = past_len + np.arange(SUB)
    zero_state = jnp.zeros((nb, RET_HEADS, RET_QK, HEAD_DIM), F32)
    zero_pre = jnp.zeros((SUB, FFN_PAD), F32)

    outs_p = [[] for _ in range(7)]
    outs_s = [[] for _ in range(7)]
    for layer in range(depth):
        w_sb, w_ret, w_dsa = _prep_in_weights(w_in[layer])
        wo = w_o[layer].astype(BF16)
        wg = _pad_cols(w_up[layer][:, :FFN_DIM], FFN_PAD).astype(BF16)
        wv = _pad_cols(w_up[layer][:, FFN_DIM:], FFN_PAD).astype(BF16)
        wd = jnp.pad(w_down[layer], ((0, FFN_PAD - FFN_DIM), (0, 0))).astype(BF16)
        cw = _pad_cols(conv_w[layer], FFN_PAD)
        cb = _pad_cols(conv_b[layer].reshape(1, -1), FFN_PAD)

        sb, ret, dsa = _in_projections(yp, g_attn[layer], w_sb, w_ret, w_dsa, g_q[layer], g_k[layer], 512, l)
        o_sb = _sb_prompt(sb[0], sb[2], sb[4], nb, l)
        o_ret, s_ret = _retention(ret[0], ret[1], ret[2], ret[3], zero_state, pos_p, BLK, BLK, nb)
        o_dsa = _dsa_prompt(rel_bias, dsa[0], dsa[5], dsa[8], dsa[2], dsa[4], dsa[7], nb, l)
        yp, hp = _out_proj(yp, o_sb, o_ret, o_dsa, wo, g_ffn[layer], 512)
        yp, gate_p = _ffn(yp, hp, wg, wv, wd, cw, cb, zero_pre, 512, None, l // 512)
        for lst, val in zip(outs_p, (tokens_first(sb[1]), tokens_first(sb[3]), tokens_first(dsa[1]),
                                     tokens_first(dsa[3]), dsa[6].reshape(nb, l, IDX_DIM), s_ret,
                                     gate_p.reshape(nb, l // 512, SUB, FFN_PAD)[:, -1, SUB - 2:, :FFN_DIM])):
            lst.append(val)

        sb, ret, dsa = _in_projections(ys, g_attn[layer], w_sb, w_ret, w_dsa, g_q[layer], g_k[layer], ts, SUB)
        r3 = lambda a: a.reshape(db, SUB, a.shape[-1])
        o_sb = _sb_decode(page_table, r3(sb[0]), sb[1], sb[3], c_sb_k, c_sb_v, layer)
        o_ret, s_ret = _retention(ret[0], ret[1], ret[2], ret[3], state_ret[layer], pos_s, SUB, ds, db)
        scores = _idx_decode(page_table, r3(dsa[5]), r3(dsa[8]), c_idx_k, layer)
        o_dsa = _dsa_decode(page_table, rel_bias, r3(dsa[0]), r3(dsa[5]), r3(dsa[8]), r3(dsa[7]), dsa[1],
                            dsa[3], scores, c_dsa_k, c_dsa_v, layer, ds)
        ys, hs = _out_proj(ys, o_sb.reshape(ts, HD_SB), o_ret, o_dsa.reshape(ts, HD_DSA), wo, g_ffn[layer], ts)
        pre = jnp.pad(state_conv[layer], ((0, 0), (0, SUB - 2), (0, FFN_PAD - FFN_DIM))).reshape(ts, FFN_PAD)
        ys, gate_s = _ffn(ys, hs, wg, wv, wd, cw, cb, pre, ts, SUB, 1)
        cut = lambda a: tokens_first(a)[:, :ds]
        for lst, val in zip(outs_s, (cut(sb[1]), cut(sb[3]), cut(dsa[1]), cut(dsa[3]),
                                     dsa[6].reshape(db, SUB, IDX_DIM)[:, :ds], s_ret,
                                     gate_s.reshape(db, SUB, FFN_PAD)[:, ds - 2:ds, :FFN_DIM])):
            lst.append(val)

    stk = lambda lst: jnp.stack(lst)
    return ((yp.reshape(nb, l, d), ys.reshape(db, SUB, d)[:, :ds])
            + tuple(stk(o) for o in outs_p) + tuple(stk(o) for o in outs_s))
```
